```python
import jax, jax.numpy as jnp
from jax import lax
import numpy as np

D_MODEL = 1024
BATCH = 8
SEQ = 2048
DEPTH = 4
DEC_BATCH = 128
DEC_SEQ = 4
PAST_LEN = 16384
PAGE_SIZE = 128

N_META = 16
N_EVEN = (DEPTH + 1) // 2
N_ODD = DEPTH // 2
D_SSM = D_MODEL // 2
SSM_GROUP = 16
N_SSM_GROUPS = D_SSM // SSM_GROUP
SSM_STATE = 64
D_SCONV = D_MODEL - D_SSM
SCONV_W = 3
D_IN_EVEN = D_SSM + 3 * D_SCONV
D_CONF = D_MODEL
CONF_W = 31
D_FF = ((8 * D_MODEL // 3 + 127) // 128) * 128
FFN_W = 3
EPS = 1e-6

kernel_name = 'hybrid_s5_shortconv_conformer_convffn_step'


def rms_norm(x, g):
    xf = x.astype(jnp.float32)
    y = xf * lax.rsqrt(jnp.mean(xf * xf, axis=-1, keepdims=True) + EPS)
    return (y * g.astype(jnp.float32)).astype(x.dtype)


def layer_norm(x, g, b):
    xf = x.astype(jnp.float32)
    mu = jnp.mean(xf, axis=-1, keepdims=True)
    xc = xf - mu
    y = xc * lax.rsqrt(jnp.mean(xc * xc, axis=-1, keepdims=True) + EPS)
    return (y * g.astype(jnp.float32) + b.astype(jnp.float32)).astype(x.dtype)


def causal_dwconv(x, buf, w):
    k = w.shape[0]
    c = x.shape[-1]
    full = jnp.concatenate([buf.astype(x.dtype), x], axis=1)
    y = lax.conv_general_dilated(full, w[:, None, :].astype(x.dtype), window_strides=(1,),
                                 padding='VALID', dimension_numbers=('NWC', 'WIO', 'NWC'),
                                 feature_group_count=c)
    return y, full[:, -(k - 1):]


def s5_scan(u, h0_re, h0_im, lam_re, lam_im, log_dt, b_re, b_im, c_re, c_im, d_skip):
    f32 = jnp.float32
    n, l, _ = u.shape
    uf = u.astype(f32).reshape(n, l, N_SSM_GROUPS, SSM_GROUP)
    lam = lax.complex(lam_re.astype(f32), lam_im.astype(f32))
    dt = jnp.exp(log_dt.astype(f32))[:, None]
    lam_bar = jnp.exp(lam * dt)
    b_bar = ((lam_bar - 1.0) / lam)[..., None] * lax.complex(b_re.astype(f32), b_im.astype(f32))
    bu = jnp.einsum('nlgi,gpi->nlgp', uf.astype(jnp.complex64), b_bar)
    h0 = lax.complex(h0_re.astype(f32), h0_im.astype(f32))
    bu = bu.at[:, 0].add(lam_bar * h0)
    a = jnp.broadcast_to(lam_bar, bu.shape)

    def combine(e1, e2):
        a1, b1 = e1
        a2, b2 = e2
        return a1 * a2, a2 * b1 + b2

    _, h = lax.associative_scan(combine, (a, bu), axis=1)
    c = lax.complex(c_re.astype(f32), c_im.astype(f32))
    y = jnp.einsum('nlgp,gip->nlgi', h, c).real + d_skip.astype(f32).reshape(N_SSM_GROUPS, SSM_GROUP) * uf
    h_last = h[:, -1]
    return y.reshape(n, l, D_SSM), jnp.real(h_last), jnp.imag(h_last)


def even_mixer(xn, h_re, h_im, sbuf, p, i):
    z = xn @ p['w_in_even'][i]
    u_a, x_b, g_b, g_c = jnp.split(z, [D_SSM, D_SSM + D_SCONV, D_SSM + 2 * D_SCONV], axis=-1)
    y_a, nh_re, nh_im = s5_scan(u_a, h_re, h_im, p['ssm_lam_re'][i], p['ssm_lam_im'][i], p['ssm_log_dt'][i],
                                p['ssm_b_re'][i], p['ssm_b_im'][i], p['ssm_c_re'][i], p['ssm_c_im'][i],
                                p['ssm_d'][i])
    y_a = jax.nn.gelu(y_a)
    y_a = y_a * jax.nn.sigmoid(y_a @ p['w_glu'][i].astype(jnp.float32) + p['b_glu'][i].astype(jnp.float32))
    conv, nbuf = causal_dwconv(g_c * x_b, sbuf, p['w_sconv'][i])
    y_b = g_b * conv
    out = jnp.concatenate([y_a.astype(xn.dtype), y_b], axis=-1) @ p['w_out_even'][i]
    return out, nh_re.astype(h_re.dtype), nh_im.astype(h_im.dtype), nbuf


def odd_mixer(xn, cbuf, p, i):
    h = xn @ p['w_conf_pw1'][i] + p['b_conf_pw1'][i]
    a, g = jnp.split(h, [D_CONF], axis=-1)
    v = a * jax.nn.sigmoid(g)
    conv, nbuf = causal_dwconv(v, cbuf, p['w_conf_dw'][i])
    conv = layer_norm(conv, p['conf_ln_g'][i], p['conf_ln_b'][i])
    out = jax.nn.silu(conv) @ p['w_conf_pw2'][i] + p['b_conf_pw2'][i]
    return out, nbuf


def conv_ffn(xn, fbuf, p, l):
    u = xn @ p['w_ffn_up'][l]
    a, g = jnp.split(u, [D_FF], axis=-1)
    a_c, nbuf = causal_dwconv(a, fbuf, p['w_ffn_conv'][l])
    return (jax.nn.silu(a_c) * g) @ p['w_ffn_down'][l], nbuf


def trunk(x, ssm_re, ssm_im, sconv, cconv, ffn_buf, p):
    n_re, n_im, n_s, n_c, n_f = [], [], [], [], []
    for l in range(DEPTH):
        i = l // 2
        xn = rms_norm(x, p['norm_mix'][l])
        if l % 2 == 0:
            out, r, im, sb = even_mixer(xn, ssm_re[i], ssm_im[i], sconv[i], p, i)
            n_re.append(r)
            n_im.append(im)
            n_s.append(sb)
        else:
            out, cb = odd_mixer(xn, cconv[i], p, i)
            n_c.append(cb)
        x = x + out
        out, fb = conv_ffn(rms_norm(x, p['norm_ffn'][l]), ffn_buf[l], p, l)
        n_f.append(fb)
        x = x + out
    y = rms_norm(x, p['norm_final'])
    return y, jnp.stack(n_re), jnp.stack(n_im), jnp.stack(n_s), jnp.stack(n_c), jnp.stack(n_f)


def setup_inputs(seed: int = 0) -> dict:
    key = jax.random.key(seed)
    ks = iter(jax.random.split(key, 48))

    def nrm(shape, scale):
        return jax.random.normal(next(ks), shape, jnp.float32) * scale

    G, P = N_SSM_GROUPS, SSM_STATE
    lam_im = jnp.broadcast_to(jnp.pi * jnp.arange(P, dtype=jnp.float32), (N_EVEN, G, P))
    return {
        'x_prompt': nrm((BATCH, SEQ, D_MODEL), 1.0),
        'x_sample': nrm((DEC_BATCH, DEC_SEQ, D_MODEL), 1.0),
        'state_ssm_re': nrm((N_EVEN, DEC_BATCH, G, P), 0.5),
        'state_ssm_im': nrm((N_EVEN, DEC_BATCH, G, P), 0.5),
        'state_sconv': nrm((N_EVEN, DEC_BATCH, SCONV_W - 1, D_SCONV), 0.5),
        'state_cconv': nrm((N_ODD, DEC_BATCH, CONF_W - 1, D_CONF), 0.5),
        'state_ffn': nrm((DEPTH, DEC_BATCH, FFN_W - 1, D_FF), 0.5),
        'meta_tokens': nrm((N_META, D_MODEL), 1.0),
        'norm_mix': 1.0 + nrm((DEPTH, D_MODEL), 0.02),
        'norm_ffn': 1.0 + nrm((DEPTH, D_MODEL), 0.02),
        'norm_final': 1.0 + nrm((D_MODEL,), 0.02),
        'w_in_even': nrm((N_EVEN, D_MODEL, D_IN_EVEN), D_MODEL ** -0.5),
        'ssm_lam_re': -0.5 + nrm((N_EVEN, G, P), 0.01),
        'ssm_lam_im': lam_im + nrm((N_EVEN, G, P), 0.01),
        'ssm_log_dt': jax.random.uniform(next(ks), (N_EVEN, G), jnp.float32, np.log(1e-3), np.log(1e-1)),
        'ssm_b_re': nrm((N_EVEN, G, P, SSM_GROUP), (2 * SSM_GROUP) ** -0.5),
        'ssm_b_im': nrm((N_EVEN, G, P, SSM_GROUP), (2 * SSM_GROUP) ** -0.5),
        'ssm_c_re': nrm((N_EVEN, G, SSM_GROUP, P), (2 * P) ** -0.5),
        'ssm_c_im': nrm((N_EVEN, G, SSM_GROUP, P), (2 * P) ** -0.5),
        'ssm_d': nrm((N_EVEN, D_SSM), 1.0),
        'w_glu': nrm((N_EVEN, D_SSM, D_SSM), D_SSM ** -0.5),
        'b_glu': nrm((N_EVEN, D_SSM), 0.02),
        'w_sconv': nrm((N_EVEN, SCONV_W, D_SCONV), SCONV_W ** -0.5),
        'w_out_even': nrm((N_EVEN, D_SSM + D_SCONV, D_MODEL), (D_SSM + D_SCONV) ** -0.5),
        'w_conf_pw1': nrm((N_ODD, D_MODEL, 2 * D_CONF), D_MODEL ** -0.5),
        'b_conf_pw1': nrm((N_ODD, 2 * D_CONF), 0.02),
        'w_conf_dw': nrm((N_ODD, CONF_W, D_CONF), CONF_W ** -0.5),
        'conf_ln_g': 1.0 + nrm((N_ODD, D_CONF), 0.02),
        'conf_ln_b': nrm((N_ODD, D_CONF), 0.02),
        'w_conf_pw2': nrm((N_ODD, D_CONF, D_MODEL), D_CONF ** -0.5),
        'b_conf_pw2': nrm((N_ODD, D_MODEL), 0.02),
        'w_ffn_up': nrm((DEPTH, D_MODEL, 2 * D_FF), D_MODEL ** -0.5),
        'w_ffn_conv': nrm((DEPTH, FFN_W, D_FF), FFN_W ** -0.5),
        'w_ffn_down': nrm((DEPTH, D_FF, D_MODEL), D_FF ** -0.5),
    }


def reference(x_prompt, x_sample, state_ssm_re, state_ssm_im, state_sconv, state_cconv, state_ffn,
              meta_tokens, norm_mix, norm_ffn, norm_final, w_in_even, ssm_lam_re, ssm_lam_im, ssm_log_dt,
              ssm_b_re, ssm_b_im, ssm_c_re, ssm_c_im, ssm_d, w_glu, b_glu, w_sconv, w_out_even,
              w_conf_pw1, b_conf_pw1, w_conf_dw, conf_ln_g, conf_ln_b, w_conf_pw2, b_conf_pw2,
              w_ffn_up, w_ffn_conv, w_ffn_down):
    p = dict(norm_mix=norm_mix, norm_ffn=norm_ffn, norm_final=norm_final, w_in_even=w_in_even,
             ssm_lam_re=ssm_lam_re, ssm_lam_im=ssm_lam_im, ssm_log_dt=ssm_log_dt,
             ssm_b_re=ssm_b_re, ssm_b_im=ssm_b_im, ssm_c_re=ssm_c_re, ssm_c_im=ssm_c_im, ssm_d=ssm_d,
             w_glu=w_glu, b_glu=b_glu, w_sconv=w_sconv, w_out_even=w_out_even,
             w_conf_pw1=w_conf_pw1, b_conf_pw1=b_conf_pw1, w_conf_dw=w_conf_dw,
             conf_ln_g=conf_ln_g, conf_ln_b=conf_ln_b, w_conf_pw2=w_conf_pw2, b_conf_pw2=b_conf_pw2,
             w_ffn_up=w_ffn_up, w_ffn_conv=w_ffn_conv, w_ffn_down=w_ffn_down)

    dt = x_prompt.dtype
    meta = jnp.broadcast_to(meta_tokens.astype(dt)[None], (BATCH, N_META, D_MODEL))
    xp = jnp.concatenate([meta, x_prompt], axis=1)
    zp_re = jnp.zeros((N_EVEN, BATCH, N_SSM_GROUPS, SSM_STATE), dt)
    zp_s = jnp.zeros((N_EVEN, BATCH, SCONV_W - 1, D_SCONV), dt)
    zp_c = jnp.zeros((N_ODD, BATCH, CONF_W - 1, D_CONF), dt)
    zp_f = jnp.zeros((DEPTH, BATCH, FFN_W - 1, D_FF), dt)
    yp, p_ssm_re, p_ssm_im, p_sconv, p_cconv, p_ffn = trunk(xp, zp_re, zp_re, zp_s, zp_c, zp_f, p)
    y_prompt = yp[:, N_META:]

    y_sample, s_ssm_re, s_ssm_im, s_sconv, s_cconv, s_ffn = trunk(
        x_sample, state_ssm_re, state_ssm_im, state_sconv, state_cconv, state_ffn, p)

    return (y_prompt, y_sample, p_ssm_re, p_ssm_im, p_sconv, p_cconv, p_ffn,
            s_ssm_re, s_ssm_im, s_sconv, s_cconv, s_ffn)
```

```python
import functools
import math

import jax
import jax.numpy as jnp
from jax import lax
from jax.experimental import pallas as pl
from jax.experimental.pallas import tpu as pltpu

D_MODEL = 1024
N_META = 16
SSM_GROUP = 16
SSM_STATE = 64
D_SSM = D_MODEL // 2
N_SSM_GROUPS = D_SSM // SSM_GROUP
D_SCONV = D_MODEL - D_SSM
SCONV_W = 3
D_IN_EVEN = D_SSM + 3 * D_SCONV
D_CONF = D_MODEL
CONF_W = 31
D_FF = ((8 * D_MODEL // 3 + 127) // 128) * 128
FFN_W = 3
EPS = 1e-6

LANES = 128
SUBLANES = 8
KB_COLS = LANES
N_KB = D_SSM // KB_COLS
KB_GROUPS = KB_COLS // SSM_GROUP
KB_STATES = KB_GROUPS * SSM_STATE
D_STATE = N_KB * 2 * KB_STATES
FFN_CHUNK = 256
VMEM_LIMIT_BYTES = 56 * 1024 * 1024

_F32 = jnp.float32
_BF16 = jnp.bfloat16


def _dot(a, b):
    return jnp.dot(a, b, preferred_element_type=_F32)


def _rms(x, g):
    return (x * lax.rsqrt(jnp.mean(x * x, axis=-1, keepdims=True) + EPS)) * g


def _sigmoid(x):
    return 1.0 / (1.0 + jnp.exp(-x))


def _gelu_tanh(x):
    c = math.sqrt(2.0 / math.pi)
    return x * (0.5 * (1.0 + jnp.tanh(c * (x + 0.044715 * (x * x * x)))))


def _shifted(carry, cur, nb, k, width):
    back = (width - 1 - k) * nb
    if back == 0:
        return cur
    tm = cur.shape[0]
    head = carry[carry.shape[0] - back:]
    if back >= tm:
        return head[:tm]
    return jnp.concatenate([head, cur[:tm - back]], axis=0)


def _short_conv(carry, cur, w, nb):
    width = w.shape[0]
    acc = None
    for k in range(width):
        term = w[k:k + 1, :] * _shifted(carry, cur, nb, k, width)
        acc = term if acc is None else acc + term
    tail = jnp.concatenate([carry, cur], axis=0)[cur.shape[0]:]
    return acc, tail


def _pick_unroll(n):
    for u in (8, 4, 2):
        if n % u == 0:
            return u
    return 1


def _even_kernel(x_ref, h0_ref, s0_ref, g_ref, win_ref, bcat_ref, lam_ref, ccat_ref, dskip_ref,
                 wglu_ref, bglu_ref, wsc_ref, wout_ref,
                 xo_ref, ho_ref, so_ref,
                 hcar_ref, scar_ref, hb_ref, *, tt, nb):
    t_idx = pl.program_id(1)
    n_t = pl.num_programs(1)
    tm = tt * nb

    @pl.when(t_idx == 0)
    def _():
        hcar_ref[...] = h0_ref[...]
        scar_ref[...] = s0_ref[...].reshape(2 * nb, D_SCONV)

    x = x_ref[...].reshape(tm, D_MODEL)
    xn = _rms(x, g_ref[...]).astype(_BF16)
    z = _dot(xn, win_ref[...])
    u = z[:, :D_SSM]
    x_b = z[:, D_SSM:D_SSM + D_SCONV]
    g_b = z[:, D_SSM + D_SCONV:D_SSM + 2 * D_SCONV]
    g_c = z[:, D_SSM + 2 * D_SCONV:]
    ub = u.astype(_BF16)

    unroll = _pick_unroll(tt)
    y_parts = []
    for kb in range(N_KB):
        c0 = kb * 2 * KB_STATES
        hb_ref[...] = _dot(ub[:, kb * KB_COLS:(kb + 1) * KB_COLS], bcat_ref[kb])
        lam_re = jnp.broadcast_to(lam_ref[kb, 0:1, :], (SUBLANES, KB_STATES))
        lam_im = jnp.broadcast_to(lam_ref[kb, 1:2, :], (SUBLANES, KB_STATES))
        for j in range(nb // SUBLANES):
            rows = slice(j * SUBLANES, (j + 1) * SUBLANES)
            h_re0 = hcar_ref[rows, c0:c0 + KB_STATES]
            h_im0 = hcar_ref[rows, c0 + KB_STATES:c0 + 2 * KB_STATES]

            def step(t, carry, j=j, lam_re=lam_re, lam_im=lam_im):
                h_re, h_im = carry
                r = pl.ds(pl.multiple_of(t * nb + j * SUBLANES, SUBLANES), SUBLANES)
                n_re = lam_re * h_re - lam_im * h_im + hb_ref[r, 0:KB_STATES]
                n_im = lam_re * h_im + lam_im * h_re + hb_ref[r, KB_STATES:2 * KB_STATES]
                hb_ref[r, 0:KB_STATES] = n_re
                hb_ref[r, KB_STATES:2 * KB_STATES] = n_im
                return n_re, n_im

            h_re, h_im = lax.fori_loop(0, tt, step, (h_re0, h_im0), unroll=unroll)
            hcar_ref[rows, c0:c0 + KB_STATES] = h_re
            hcar_ref[rows, c0 + KB_STATES:c0 + 2 * KB_STATES] = h_im
        y_parts.append(_dot(hb_ref[...].astype(_BF16), ccat_ref[kb]))
    y = jnp.concatenate(y_parts, axis=1) + dskip_ref[...] * u

    y_a = _gelu_tanh(y)
    y_a = y_a * _sigmoid(_dot(y_a.astype(_BF16), wglu_ref[...]) + bglu_ref[...])

    conv, tail = _short_conv(scar_ref[...], g_c * x_b, wsc_ref[...], nb)
    scar_ref[...] = tail
    y_b = g_b * conv

    mixed = jnp.concatenate([y_a.astype(_BF16), y_b.astype(_BF16)], axis=1)
    xo_ref[...] = (x + _dot(mixed, wout_ref[...])).reshape(tt, nb, D_MODEL)

    @pl.when(t_idx == n_t - 1)
    def _():
        ho_ref[...] = hcar_ref[...]
        so_ref[...] = scar_ref[...].reshape(2, nb, D_SCONV)


def _odd_kernel(x_ref, c0_ref, g_ref, w1_ref, b1_ref, wdw_ref, lng_ref, lnb_ref, w2_ref, b2_ref,
                xo_ref, co_ref,
                win_ref, act_ref, *, tt, nb, n_t, rb):
    t_idx = pl.program_id(1)
    tm = tt * nb
    halo = (CONF_W - 1) * nb

    @pl.when(t_idx == 0)
    def _():
        win_ref[0:halo, :] = c0_ref[...].reshape(halo, D_CONF)

    x = x_ref[...].reshape(tm, D_MODEL)
    xn = _rms(x, g_ref[...]).astype(_BF16)
    h = _dot(xn, w1_ref[...]) + b1_ref[...]
    win_ref[halo:halo + tm, :] = h[:, :D_CONF] * _sigmoid(h[:, D_CONF:])

    def block(i, _):
        r0 = pl.multiple_of(i * rb, SUBLANES)
        acc = wdw_ref[0:1, :] * win_ref[pl.ds(r0, rb), :]
        for k in range(1, CONF_W):
            acc = acc + wdw_ref[k:k + 1, :] * win_ref[pl.ds(r0 + k * nb, rb), :]
        mu = jnp.mean(acc, axis=-1, keepdims=True)
        xc = acc - mu
        yv = xc * lax.rsqrt(jnp.mean(xc * xc, axis=-1, keepdims=True) + EPS)
        yv = yv * lng_ref[...] + lnb_ref[...]
        act_ref[pl.ds(r0, rb), :] = (yv * _sigmoid(yv)).astype(_BF16)
        return 0

    lax.fori_loop(0, tm // rb, block, 0)

    out = _dot(act_ref[...], w2_ref[...]) + b2_ref[...]
    xo_ref[...] = (x + out).reshape(tt, nb, D_MODEL)

    @pl.when(t_idx == n_t - 1)
    def _():
        co_ref[...] = win_ref[tm:tm + halo, :].reshape(CONF_W - 1, nb, D_CONF)

    if n_t > 1:
        assert tt >= CONF_W - 1

        @pl.when(t_idx < n_t - 1)
        def _():
            win_ref[0:halo, :] = win_ref[tm:tm + halo, :]


def _ffn_kernel(x_ref, f0_ref, g_ref, wa_ref, wg_ref, wc_ref, wd_ref, gfin_ref,
                xo_ref, fo_ref,
                fcar_ref, acc_ref, *, tt, nb, final_norm):
    t_idx = pl.program_id(1)
    n_t = pl.num_programs(1)
    tm = tt * nb

    @pl.when(t_idx == 0)
    def _():
        fcar_ref[...] = f0_ref[...].reshape(2 * nb, D_FF)

    x = x_ref[...].reshape(tm, D_MODEL)
    xn = _rms(x, g_ref[...]).astype(_BF16)
    for c in range(D_FF // FFN_CHUNK):
        cols = slice(c * FFN_CHUNK, (c + 1) * FFN_CHUNK)
        a = _dot(xn, wa_ref[:, cols])
        gate = _dot(xn, wg_ref[:, cols])
        a_c, tail = _short_conv(fcar_ref[:, cols], a, wc_ref[:, cols], nb)
        fcar_ref[:, cols] = tail
        hidden = ((a_c * _sigmoid(a_c)) * gate).astype(_BF16)
        part = _dot(hidden, wd_ref[cols, :])
        if c == 0:
            acc_ref[...] = part
        else:
            acc_ref[...] += part
    y = x + acc_ref[...]
    if final_norm:
        y = _rms(y, gfin_ref[...])
    xo_ref[...] = y.reshape(tt, nb, D_MODEL)

    @pl.when(t_idx == n_t - 1)
    def _():
        fo_ref[...] = fcar_ref[...].reshape(2, nb, D_FF)


def _const_spec(shape):
    zeros = (0,) * len(shape)
    return pl.BlockSpec(shape, lambda b, t: zeros, pipeline_mode=pl.Buffered(1))


def _x_spec(tt, nbt):
    return pl.BlockSpec((tt, nbt, D_MODEL), lambda b, t: (t, b, 0))


def _state_spec(rows, nbt, width):
    return pl.BlockSpec((rows, nbt, width), lambda b, t: (0, b, 0))


_PARAMS = pltpu.CompilerParams(dimension_semantics=("arbitrary", "arbitrary"),
                               vmem_limit_bytes=VMEM_LIMIT_BYTES)


def _even_call(x, h0, s0, w, *, tt, nbt):
    length, nb, _ = x.shape
    grid = (nb // nbt, length // tt)
    weights = (w["g"], w["w_in"], w["bcat"], w["lam"], w["ccat"], w["dskip"], w["w_glu"],
               w["b_glu"], w["w_sconv"], w["w_out"])
    return pl.pallas_call(
        functools.partial(_even_kernel, tt=tt, nb=nbt),
        grid=grid,
        in_specs=[_x_spec(tt, nbt),
                  pl.BlockSpec((nbt, D_STATE), lambda b, t: (b, 0)),
                  _state_spec(SCONV_W - 1, nbt, D_SCONV)] + [_const_spec(a.shape) for a in weights],
        out_specs=[_x_spec(tt, nbt),
                   pl.BlockSpec((nbt, D_STATE), lambda b, t: (b, 0)),
                   _state_spec(SCONV_W - 1, nbt, D_SCONV)],
        out_shape=[jax.ShapeDtypeStruct(x.shape, _F32),
                   jax.ShapeDtypeStruct(h0.shape, _F32),
                   jax.ShapeDtypeStruct(s0.shape, _F32)],
        scratch_shapes=[pltpu.VMEM((nbt, D_STATE), _F32),
                        pltpu.VMEM(((SCONV_W - 1) * nbt, D_SCONV), _F32),
                        pltpu.VMEM((tt * nbt, 2 * KB_STATES), _F32)],
        compiler_params=_PARAMS,
        name="even_mixer",
    )(x, h0, s0, *weights)


def _odd_call(x, c0, w, *, tt, nbt, rb):
    length, nb, _ = x.shape
    n_t = length // tt
    grid = (nb // nbt, n_t)
    weights = (w["g"], w["w1"], w["b1"], w["w_dw"], w["ln_g"], w["ln_b"], w["w2"], w["b2"])
    return pl.pallas_call(
        functools.partial(_odd_kernel, tt=tt, nb=nbt, n_t=n_t, rb=rb),
        grid=grid,
        in_specs=[_x_spec(tt, nbt), _state_spec(CONF_W - 1, nbt, D_CONF)]
        + [_const_spec(a.shape) for a in weights],
        out_specs=[_x_spec(tt, nbt), _state_spec(CONF_W - 1, nbt, D_CONF)],
        out_shape=[jax.ShapeDtypeStruct(x.shape, _F32), jax.ShapeDtypeStruct(c0.shape, _F32)],
        scratch_shapes=[pltpu.VMEM(((CONF_W - 1 + tt) * nbt, D_CONF), _F32),
                        pltpu.VMEM((tt * nbt, D_CONF), _BF16)],
        compiler_params=_PARAMS,
        name="odd_mixer",
    )(x, c0, *weights)


def _ffn_call(x, f0, w, g_final, *, tt, nbt, final_norm):
    length, nb, _ = x.shape
    grid = (nb // nbt, length // tt)
    weights = (w["g"], w["w_a"], w["w_g"], w["w_conv"], w["w_down"], g_final)
    return pl.pallas_call(
        functools.partial(_ffn_kernel, tt=tt, nb=nbt, final_norm=final_norm),
        grid=grid,
        in_specs=[_x_spec(tt, nbt), _state_spec(FFN_W - 1, nbt, D_FF)]
        + [_const_spec(a.shape) for a in weights],
        out_specs=[_x_spec(tt, nbt), _state_spec(FFN_W - 1, nbt, D_FF)],
        out_shape=[jax.ShapeDtypeStruct(x.shape, _F32), jax.ShapeDtypeStruct(f0.shape, _F32)],
        scratch_shapes=[pltpu.VMEM(((FFN_W - 1) * nbt, D_FF), _F32),
                        pltpu.VMEM((tt * nbt, D_MODEL), _F32)],
        compiler_params=_PARAMS,
        name="conv_ffn",
    )(x, f0, *weights)


def _row(v):
    return v.reshape(1, -1).astype(_F32)


def _prep_even(i, p):
    g_, p_, s_ = N_SSM_GROUPS, SSM_STATE, SSM_GROUP
    lam = lax.complex(p["ssm_lam_re"][i].astype(_F32), p["ssm_lam_im"][i].astype(_F32))
    dt = jnp.exp(p["ssm_log_dt"][i].astype(_F32))[:, None]
    lam_bar = jnp.exp(lam * dt)
    b_bar = ((lam_bar - 1.0) / lam)[..., None] * lax.complex(
        p["ssm_b_re"][i].astype(_F32), p["ssm_b_im"][i].astype(_F32))
    eye = jnp.eye(KB_GROUPS, dtype=_F32)

    def b_block(v):
        v = v.reshape(N_KB, KB_GROUPS, p_, s_)
        return jnp.einsum("kgpi,gh->kgihp", v, eye).reshape(N_KB, KB_COLS, KB_STATES)

    def c_block(v):
        v = v.reshape(N_KB, KB_GROUPS, s_, p_)
        return jnp.einsum("kgip,gh->kgphi", v, eye).reshape(N_KB, KB_STATES, KB_COLS)

    bcat = jnp.concatenate([b_block(jnp.real(b_bar)), b_block(jnp.imag(b_bar))], axis=2)
    ccat = jnp.concatenate([c_block(p["ssm_c_re"][i].astype(_F32)),
                            -c_block(p["ssm_c_im"][i].astype(_F32))], axis=1)
    lamcat = jnp.stack([jnp.real(lam_bar).reshape(N_KB, KB_STATES),
                        jnp.imag(lam_bar).reshape(N_KB, KB_STATES)], axis=1)
    del g_
    return dict(g=None, w_in=p["w_in_even"][i].astype(_BF16), bcat=bcat.astype(_BF16), lam=lamcat,
                ccat=ccat.astype(_BF16), dskip=_row(p["ssm_d"][i]),
                w_glu=p["w_glu"][i].astype(_BF16), b_glu=_row(p["b_glu"][i]),
                w_sconv=p["w_sconv"][i].astype(_F32), w_out=p["w_out_even"][i].astype(_BF16))


def _prep_odd(i, p):
    return dict(g=None, w1=p["w_conf_pw1"][i].astype(_BF16), b1=_row(p["b_conf_pw1"][i]),
                w_dw=p["w_conf_dw"][i].astype(_F32), ln_g=_row(p["conf_ln_g"][i]),
                ln_b=_row(p["conf_ln_b"][i]), w2=p["w_conf_pw2"][i].astype(_BF16),
                b2=_row(p["b_conf_pw2"][i]))


def _prep_ffn(l, p):
    up = p["w_ffn_up"][l]
    return dict(g=_row(p["norm_ffn"][l]), w_a=up[:, :D_FF].astype(_BF16),
                w_g=up[:, D_FF:].astype(_BF16), w_conv=p["w_ffn_conv"][l].astype(_F32),
                w_down=p["w_ffn_down"][l].astype(_BF16))


def _ssm_to_flat(re, im):
    nb = re.shape[0]
    re = re.reshape(nb, N_KB, KB_STATES)
    im = im.reshape(nb, N_KB, KB_STATES)
    return jnp.concatenate([re, im], axis=2).reshape(nb, D_STATE).astype(_F32)


def _flat_to_ssm(h):
    nb = h.shape[0]
    h = h.reshape(nb, N_KB, 2, KB_STATES)
    shape = (nb, N_SSM_GROUPS, SSM_STATE)
    return h[:, :, 0].reshape(shape), h[:, :, 1].reshape(shape)


def _trunk(x, ssm_re, ssm_im, sconv, cconv, ffn_buf, layers, g_final, *, tt, nbt, nbt_odd, rb):
    depth = len(layers)
    n_re, n_im, n_s, n_c, n_f = [], [], [], [], []
    for l, (mix, ffn) in enumerate(layers):
        i = l // 2
        if l % 2 == 0:
            h0 = _ssm_to_flat(ssm_re[i], ssm_im[i])
            s0 = jnp.transpose(sconv[i], (1, 0, 2))
            x, h1, s1 = _even_call(x, h0, s0, mix, tt=tt, nbt=nbt)
            re, im = _flat_to_ssm(h1)
            n_re.append(re)
            n_im.append(im)
            n_s.append(jnp.transpose(s1, (1, 0, 2)))
        else:
            c0 = jnp.transpose(cconv[i], (1, 0, 2))
            x, c1 = _odd_call(x, c0, mix, tt=tt, nbt=nbt_odd, rb=rb)
            n_c.append(jnp.transpose(c1, (1, 0, 2)))
        f0 = jnp.transpose(ffn_buf[l], (1, 0, 2))
        x, f1 = _ffn_call(x, f0, ffn, g_final, tt=tt, nbt=nbt, final_norm=(l == depth - 1))
        n_f.append(jnp.transpose(f1, (1, 0, 2)))
    return x, jnp.stack(n_re), jnp.stack(n_im), jnp.stack(n_s), jnp.stack(n_c), jnp.stack(n_f)


def kernel(x_prompt, x_sample, state_ssm_re, state_ssm_im, state_sconv, state_cconv, state_ffn, meta_tokens, norm_mix, norm_ffn, norm_final, w_in_even, ssm_lam_re, ssm_lam_im, ssm_log_dt, ssm_b_re, ssm_b_im, ssm_c_re, ssm_c_im, ssm_d, w_glu, b_glu, w_sconv, w_out_even, w_conf_pw1, b_conf_pw1, w_conf_dw, conf_ln_g, conf_ln_b, w_conf_pw2, b_conf_pw2, w_ffn_up, w_ffn_conv, w_ffn_down):
    p = dict(norm_ffn=norm_ffn, w_in_even=w_in_even, ssm_lam_re=ssm_lam_re, ssm_lam_im=ssm_lam_im,
             ssm_log_dt=ssm_log_dt, ssm_b_re=ssm_b_re, ssm_b_im=ssm_b_im, ssm_c_re=ssm_c_re,
             ssm_c_im=ssm_c_im, ssm_d=ssm_d, w_glu=w_glu, b_glu=b_glu, w_sconv=w_sconv,
             w_out_even=w_out_even, w_conf_pw1=w_conf_pw1, b_conf_pw1=b_conf_pw1,
             w_conf_dw=w_conf_dw, conf_ln_g=conf_ln_g, conf_ln_b=conf_ln_b, w_conf_pw2=w_conf_pw2,
             b_conf_pw2=b_conf_pw2, w_ffn_up=w_ffn_up, w_ffn_conv=w_ffn_conv, w_ffn_down=w_ffn_down)
    depth = norm_mix.shape[0]
    layers = []
    for l in range(depth):
        mix = _prep_even(l // 2, p) if l % 2 == 0 else _prep_odd(l // 2, p)
        mix["g"] = _row(norm_mix[l])
        layers.append((mix, _prep_ffn(l, p)))
    g_final = _row(norm_final)

    batch, seq, _ = x_prompt.shape
    dec_batch, dec_seq, _ = x_sample.shape
    dt = x_prompt.dtype
    n_even, n_odd = (depth + 1) // 2, depth // 2

    meta = jnp.broadcast_to(meta_tokens.astype(dt)[:, None, :], (N_META, batch, D_MODEL))
    xp = jnp.concatenate([meta, jnp.transpose(x_prompt, (1, 0, 2))], axis=0)
    zp_h = jnp.zeros((n_even, batch, N_SSM_GROUPS, SSM_STATE), dt)
    zp_s = jnp.zeros((n_even, batch, SCONV_W - 1, D_SCONV), dt)
    zp_c = jnp.zeros((n_odd, batch, CONF_W - 1, D_CONF), dt)
    zp_f = jnp.zeros((depth, batch, FFN_W - 1, D_FF), dt)
    yp, *p_states = _trunk(xp, zp_h, zp_h, zp_s, zp_c, zp_f, layers, g_final,
                           tt=86, nbt=batch, nbt_odd=batch, rb=16)
    y_prompt = jnp.transpose(yp[N_META:], (1, 0, 2))

    xs = jnp.transpose(x_sample, (1, 0, 2))
    ys, *s_states = _trunk(xs, state_ssm_re, state_ssm_im, state_sconv, state_cconv, state_ffn,
                           layers, g_final, tt=dec_seq, nbt=dec_batch, nbt_odd=32, rb=32)
    y_sample = jnp.transpose(ys, (1, 0, 2))

    return (y_prompt, y_sample, *p_states, *s_states)
```

```python
import functools
import math

import jax
import jax.numpy as jnp
from jax import lax
from jax.experimental import pallas as pl
from jax.experimental.pallas import tpu as pltpu

D_MODEL = 1024
N_META = 16
SSM_GROUP = 16
SSM_STATE = 64
D_SSM = D_MODEL // 2
N_SSM_GROUPS = D_SSM // SSM_GROUP
D_SCONV = D_MODEL - D_SSM
SCONV_W = 3
D_IN_EVEN = D_SSM + 3 * D_SCONV
D_CONF = D_MODEL
CONF_W = 31
D_FF = ((8 * D_MODEL // 3 + 127) // 128) * 128
FFN_W = 3
EPS = 1e-6

LANES = 128
SUBLANES = 8
KB_COLS = LANES
N_KB = D_SSM // KB_COLS
KB_GROUPS = KB_COLS // SSM_GROUP
KB_STATES = KB_GROUPS * SSM_STATE
D_STATE = N_KB * 2 * KB_STATES
FFN_CHUNK = 256
VMEM_LIMIT_BYTES = 56 * 1024 * 1024
PROMPT_TILE_STEPS = 86
SAMPLE_ODD_BATCH_TILE = 32
CONV_BLOCK_TILES = 8

_F32 = jnp.float32
_BF16 = jnp.bfloat16


def _dot(a, b):
    return jnp.dot(a, b, preferred_element_type=_F32)


def _rms(x, g):
    return (x * lax.rsqrt(jnp.mean(x * x, axis=-1, keepdims=True) + EPS)) * g


def _sigmoid(x):
    return 1.0 / (1.0 + jnp.exp(-x))


def _gelu_tanh(x):
    c = math.sqrt(2.0 / math.pi)
    return x * (0.5 * (1.0 + jnp.tanh(c * (x + 0.044715 * (x * x * x)))))


def _shifted(carry, cur, nb, k, width):
    back = (width - 1 - k) * nb
    if back == 0:
        return cur
    tm = cur.shape[0]
    head = carry[carry.shape[0] - back:]
    if back >= tm:
        return head[:tm]
    return jnp.concatenate([head, cur[:tm - back]], axis=0)


def _short_conv(carry, cur, w, nb):
    width = w.shape[0]
    acc = None
    for k in range(width):
        term = w[k:k + 1, :] * _shifted(carry, cur, nb, k, width)
        acc = term if acc is None else acc + term
    tail = jnp.concatenate([carry, cur], axis=0)[cur.shape[0]:]
    return acc, tail


def _pick_unroll(n):
    for u in (8, 4, 2):
        if n % u == 0:
            return u
    return 1


def _even_kernel(x_ref, h0_ref, s0_ref, g_ref, win_ref, bcat_ref, lam_ref, ccat_ref, dskip_ref,
                 wglu_ref, bglu_ref, wsc_ref, wout_ref,
                 xo_ref, ho_ref, so_ref,
                 hcar_ref, scar_ref, hb_ref, *, tt, nb):
    t_idx = pl.program_id(1)
    n_t = pl.num_programs(1)
    tm = tt * nb

    @pl.when(t_idx == 0)
    def _():
        hcar_ref[...] = h0_ref[...]
        scar_ref[...] = s0_ref[...].reshape(2 * nb, D_SCONV)

    x = x_ref[...].reshape(tm, D_MODEL)
    xn = _rms(x, g_ref[...]).astype(_BF16)
    z = _dot(xn, win_ref[...])
    u = z[:, :D_SSM]
    x_b = z[:, D_SSM:D_SSM + D_SCONV]
    g_b = z[:, D_SSM + D_SCONV:D_SSM + 2 * D_SCONV]
    g_c = z[:, D_SSM + 2 * D_SCONV:]
    ub = u.astype(_BF16)

    unroll = _pick_unroll(tt)
    y_parts = []
    for kb in range(N_KB):
        c0 = kb * 2 * KB_STATES
        hb_ref[...] = _dot(ub[:, kb * KB_COLS:(kb + 1) * KB_COLS], bcat_ref[kb])
        lam_re = jnp.broadcast_to(lam_ref[kb, 0:1, :], (SUBLANES, KB_STATES))
        lam_im = jnp.broadcast_to(lam_ref[kb, 1:2, :], (SUBLANES, KB_STATES))
        for j in range(nb // SUBLANES):
            rows = slice(j * SUBLANES, (j + 1) * SUBLANES)
            h_re0 = hcar_ref[rows, c0:c0 + KB_STATES]
            h_im0 = hcar_ref[rows, c0 + KB_STATES:c0 + 2 * KB_STATES]

            def step(t, carry, j=j, lam_re=lam_re, lam_im=lam_im):
                h_re, h_im = carry
                r = pl.ds(pl.multiple_of(t * nb + j * SUBLANES, SUBLANES), SUBLANES)
                n_re = lam_re * h_re - lam_im * h_im + hb_ref[r, 0:KB_STATES]
                n_im = lam_re * h_im + lam_im * h_re + hb_ref[r, KB_STATES:2 * KB_STATES]
                hb_ref[r, 0:KB_STATES] = n_re
                hb_ref[r, KB_STATES:2 * KB_STATES] = n_im
                return n_re, n_im

            h_re, h_im = lax.fori_loop(0, tt, step, (h_re0, h_im0), unroll=unroll)
            hcar_ref[rows, c0:c0 + KB_STATES] = h_re
            hcar_ref[rows, c0 + KB_STATES:c0 + 2 * KB_STATES] = h_im
        y_parts.append(_dot(hb_ref[...].astype(_BF16), ccat_ref[kb]))
    y = jnp.concatenate(y_parts, axis=1) + dskip_ref[...] * u

    y_a = _gelu_tanh(y)
    y_a = y_a * _sigmoid(_dot(y_a.astype(_BF16), wglu_ref[...]) + bglu_ref[...])

    conv, tail = _short_conv(scar_ref[...], g_c * x_b, wsc_ref[...], nb)
    scar_ref[...] = tail
    y_b = g_b * conv

    mixed = jnp.concatenate([y_a.astype(_BF16), y_b.astype(_BF16)], axis=1)
    xo_ref[...] = (x + _dot(mixed, wout_ref[...])).reshape(tt, nb, D_MODEL)

    @pl.when(t_idx == n_t - 1)
    def _():
        ho_ref[...] = hcar_ref[...]
        so_ref[...] = scar_ref[...].reshape(2, nb, D_SCONV)


def _odd_kernel(x_ref, c0_ref, g_ref, w1_ref, b1_ref, wdw_ref, lng_ref, lnb_ref, w2_ref, b2_ref,
                xo_ref, co_ref,
                win_ref, conv_ref, *, tt, nb, n_t):
    t_idx = pl.program_id(1)
    tm = tt * nb
    halo = (CONF_W - 1) * nb

    @pl.when(t_idx == 0)
    def _():
        win_ref[0:halo, :] = c0_ref[...].reshape(halo, D_CONF)

    x = x_ref[...].reshape(tm, D_MODEL)
    xn = _rms(x, g_ref[...]).astype(_BF16)
    h = _dot(xn, w1_ref[...]) + b1_ref[...]
    win_ref[halo:halo + tm, :] = h[:, :D_CONF] * _sigmoid(h[:, D_CONF:])

    def conv_block(r0, n_tiles):
        view = win_ref.at[pl.ds(r0, halo + n_tiles * SUBLANES), :]
        for c in range(D_CONF // LANES):
            lanes = slice(c * LANES, (c + 1) * LANES)
            accs = [None] * n_tiles
            for k in range(CONF_W):
                tap = wdw_ref[k * SUBLANES:(k + 1) * SUBLANES, lanes]
                for j in range(n_tiles):
                    lo = k * nb + j * SUBLANES
                    term = tap * view[lo:lo + SUBLANES, lanes]
                    accs[j] = term if accs[j] is None else accs[j] + term
            for j in range(n_tiles):
                conv_ref[pl.ds(r0 + j * SUBLANES, SUBLANES), lanes] = accs[j]

    n_row_tiles = tm // SUBLANES
    block_rows = CONV_BLOCK_TILES * SUBLANES

    def conv_trip(i, _):
        conv_block(pl.multiple_of(i * block_rows, block_rows), CONV_BLOCK_TILES)
        return 0

    lax.fori_loop(0, n_row_tiles // CONV_BLOCK_TILES, conv_trip, 0)
    if n_row_tiles % CONV_BLOCK_TILES:
        conv_block(n_row_tiles // CONV_BLOCK_TILES * block_rows, n_row_tiles % CONV_BLOCK_TILES)

    conv = conv_ref[...]
    mu = jnp.mean(conv, axis=-1, keepdims=True)
    xc = conv - mu
    yv = xc * lax.rsqrt(jnp.mean(xc * xc, axis=-1, keepdims=True) + EPS)
    yv = yv * lng_ref[...] + lnb_ref[...]
    act = (yv * _sigmoid(yv)).astype(_BF16)

    out = _dot(act, w2_ref[...]) + b2_ref[...]
    xo_ref[...] = (x + out).reshape(tt, nb, D_MODEL)

    @pl.when(t_idx == n_t - 1)
    def _():
        co_ref[...] = win_ref[tm:tm + halo, :].reshape(CONF_W - 1, nb, D_CONF)

    if n_t > 1:
        assert tt >= CONF_W - 1

        @pl.when(t_idx < n_t - 1)
        def _():
            win_ref[0:halo, :] = win_ref[tm:tm + halo, :]


def _ffn_kernel(x_ref, f0_ref, g_ref, wup_ref, wc_ref, wd_ref, gfin_ref,
                xo_ref, fo_ref,
                fcar_ref, acc_ref, *, tt, nb, final_norm):
    t_idx = pl.program_id(1)
    n_t = pl.num_programs(1)
    tm = tt * nb

    @pl.when(t_idx == 0)
    def _():
        fcar_ref[...] = f0_ref[...].reshape(2 * nb, D_FF)

    x = x_ref[...].reshape(tm, D_MODEL)
    xn = _rms(x, g_ref[...]).astype(_BF16)
    for c in range(D_FF // FFN_CHUNK):
        cols = slice(c * FFN_CHUNK, (c + 1) * FFN_CHUNK)
        a = _dot(xn, wup_ref[:, cols])
        gate = _dot(xn, wup_ref[:, D_FF + c * FFN_CHUNK:D_FF + (c + 1) * FFN_CHUNK])
        a_c, tail = _short_conv(fcar_ref[:, cols], a, wc_ref[:, cols], nb)
        fcar_ref[:, cols] = tail
        hidden = ((a_c * _sigmoid(a_c)) * gate).astype(_BF16)
        part = _dot(hidden, wd_ref[cols, :])
        if c == 0:
            acc_ref[...] = part
        else:
            acc_ref[...] += part
    y = x + acc_ref[...]
    if final_norm:
        y = _rms(y, gfin_ref[...])
    xo_ref[...] = y.reshape(tt, nb, D_MODEL)

    @pl.when(t_idx == n_t - 1)
    def _():
        fo_ref[...] = fcar_ref[...].reshape(2, nb, D_FF)


def _layer_spec(arr, layer):
    tail = (0,) * (arr.ndim - 1)
    return pl.BlockSpec((None,) + arr.shape[1:], lambda b, t: (layer,) + tail,
                        pipeline_mode=pl.Buffered(1))


def _x_spec(tt, nbt):
    return pl.BlockSpec((tt, nbt, D_MODEL), lambda b, t: (t, b, 0))


def _state_spec(rows, nbt, width):
    return pl.BlockSpec((rows, nbt, width), lambda b, t: (0, b, 0))


_PARAMS = pltpu.CompilerParams(dimension_semantics=("arbitrary", "arbitrary"),
                               vmem_limit_bytes=VMEM_LIMIT_BYTES)

_EVEN_KEYS = ("w_in", "bcat", "lam", "ccat", "dskip", "w_glu", "b_glu", "w_sconv", "w_out")
_ODD_KEYS = ("w1", "b1", "w_dw", "ln_g", "ln_b", "w2", "b2")
_FFN_KEYS = ("g_ffn", "w_up", "w_conv", "w_down")


def _even_call(x, h0, s0, w, layer, *, tt, nbt):
    length, nb, _ = x.shape
    grid = (nb // nbt, length // tt)
    w_specs = [_layer_spec(w["g_mix"], layer)] + [_layer_spec(w[k], layer // 2) for k in _EVEN_KEYS]
    return pl.pallas_call(
        functools.partial(_even_kernel, tt=tt, nb=nbt),
        grid=grid,
        in_specs=[_x_spec(tt, nbt),
                  pl.BlockSpec((nbt, D_STATE), lambda b, t: (b, 0)),
                  _state_spec(SCONV_W - 1, nbt, D_SCONV)] + w_specs,
        out_specs=[_x_spec(tt, nbt),
                   pl.BlockSpec((nbt, D_STATE), lambda b, t: (b, 0)),
                   _state_spec(SCONV_W - 1, nbt, D_SCONV)],
        out_shape=[jax.ShapeDtypeStruct(x.shape, _F32),
                   jax.ShapeDtypeStruct(h0.shape, _F32),
                   jax.ShapeDtypeStruct(s0.shape, _F32)],
        scratch_shapes=[pltpu.VMEM((nbt, D_STATE), _F32),
                        pltpu.VMEM(((SCONV_W - 1) * nbt, D_SCONV), _F32),
                        pltpu.VMEM((tt * nbt, 2 * KB_STATES), _F32)],
        compiler_params=_PARAMS,
        name="even_mixer",
    )(x, h0, s0, w["g_mix"], *[w[k] for k in _EVEN_KEYS])


def _odd_call(x, c0, w, layer, *, tt, nbt):
    length, nb, _ = x.shape
    n_t = length // tt
    grid = (nb // nbt, n_t)
    w_specs = [_layer_spec(w["g_mix"], layer)] + [_layer_spec(w[k], layer // 2) for k in _ODD_KEYS]
    return pl.pallas_call(
        functools.partial(_odd_kernel, tt=tt, nb=nbt, n_t=n_t),
        grid=grid,
        in_specs=[_x_spec(tt, nbt), _state_spec(CONF_W - 1, nbt, D_CONF)] + w_specs,
        out_specs=[_x_spec(tt, nbt), _state_spec(CONF_W - 1, nbt, D_CONF)],
        out_shape=[jax.ShapeDtypeStruct(x.shape, _F32), jax.ShapeDtypeStruct(c0.shape, _F32)],
        scratch_shapes=[pltpu.VMEM(((CONF_W - 1 + tt) * nbt, D_CONF), _F32),
                        pltpu.VMEM((tt * nbt, D_CONF), _F32)],
        compiler_params=_PARAMS,
        name="odd_mixer",
    )(x, c0, w["g_mix"], *[w[k] for k in _ODD_KEYS])


def _ffn_call(x, f0, w, layer, *, tt, nbt, final_norm):
    length, nb, _ = x.shape
    grid = (nb // nbt, length // tt)
    w_specs = [_layer_spec(w[k], layer) for k in _FFN_KEYS] + [_layer_spec(w["g_final"], 0)]
    return pl.pallas_call(
        functools.partial(_ffn_kernel, tt=tt, nb=nbt, final_norm=final_norm),
        grid=grid,
        in_specs=[_x_spec(tt, nbt), _state_spec(FFN_W - 1, nbt, D_FF)] + w_specs,
        out_specs=[_x_spec(tt, nbt), _state_spec(FFN_W - 1, nbt, D_FF)],
        out_shape=[jax.ShapeDtypeStruct(x.shape, _F32), jax.ShapeDtypeStruct(f0.shape, _F32)],
        scratch_shapes=[pltpu.VMEM(((FFN_W - 1) * nbt, D_FF), _F32),
                        pltpu.VMEM((tt * nbt, D_MODEL), _F32)],
        compiler_params=_PARAMS,
        name="conv_ffn",
    )(x, f0, *[w[k] for k in _FFN_KEYS], w["g_final"])


def _rows(v):
    return v.reshape(v.shape[0], 1, v.shape[-1]).astype(_F32)


def _prep_params(p):
    n_even = p["w_in_even"].shape[0]
    p_, s_ = SSM_STATE, SSM_GROUP
    lam = lax.complex(p["ssm_lam_re"].astype(_F32), p["ssm_lam_im"].astype(_F32))
    dt = jnp.exp(p["ssm_log_dt"].astype(_F32))[..., None]
    lam_bar = jnp.exp(lam * dt)
    b_bar = ((lam_bar - 1.0) / lam)[..., None] * lax.complex(
        p["ssm_b_re"].astype(_F32), p["ssm_b_im"].astype(_F32))
    eye = jnp.eye(KB_GROUPS, dtype=_F32)

    def b_block(v):
        v = v.reshape(n_even, N_KB, KB_GROUPS, p_, s_)
        return jnp.einsum("ekgpi,gh->ekgihp", v, eye).reshape(n_even, N_KB, KB_COLS, KB_STATES)

    def c_block(v):
        v = v.reshape(n_even, N_KB, KB_GROUPS, s_, p_)
        return jnp.einsum("ekgip,gh->ekgphi", v, eye).reshape(n_even, N_KB, KB_STATES, KB_COLS)

    bcat = jnp.concatenate([b_block(jnp.real(b_bar)), b_block(jnp.imag(b_bar))], axis=3)
    ccat = jnp.concatenate([c_block(p["ssm_c_re"].astype(_F32)),
                            -c_block(p["ssm_c_im"].astype(_F32))], axis=2)
    lamcat = jnp.stack([jnp.real(lam_bar).reshape(n_even, N_KB, KB_STATES),
                        jnp.imag(lam_bar).reshape(n_even, N_KB, KB_STATES)], axis=2)
    return dict(
        g_mix=_rows(p["norm_mix"]), g_ffn=_rows(p["norm_ffn"]), g_final=_rows(p["norm_final"][None]),
        w_in=p["w_in_even"].astype(_BF16), bcat=bcat.astype(_BF16), lam=lamcat,
        ccat=ccat.astype(_BF16), dskip=_rows(p["ssm_d"]), w_glu=p["w_glu"].astype(_BF16),
        b_glu=_rows(p["b_glu"]), w_sconv=p["w_sconv"].astype(_F32),
        w_out=p["w_out_even"].astype(_BF16),
        w1=p["w_conf_pw1"].astype(_BF16), b1=_rows(p["b_conf_pw1"]),
        w_dw=jnp.repeat(p["w_conf_dw"].astype(_F32), SUBLANES, axis=1),
        ln_g=_rows(p["conf_ln_g"]), ln_b=_rows(p["conf_ln_b"]),
        w2=p["w_conf_pw2"].astype(_BF16), b2=_rows(p["b_conf_pw2"]),
        w_up=p["w_ffn_up"].astype(_BF16), w_conv=p["w_ffn_conv"].astype(_F32),
        w_down=p["w_ffn_down"].astype(_BF16))


def _ssm_to_flat(re, im):
    nb = re.shape[0]
    re = re.reshape(nb, N_KB, KB_STATES)
    im = im.reshape(nb, N_KB, KB_STATES)
    return jnp.concatenate([re, im], axis=2).reshape(nb, D_STATE).astype(_F32)


def _flat_to_ssm(h):
    nb = h.shape[0]
    h = h.reshape(nb, N_KB, 2, KB_STATES)
    shape = (nb, N_SSM_GROUPS, SSM_STATE)
    return h[:, :, 0].reshape(shape), h[:, :, 1].reshape(shape)


def _trunk(x, ssm_re, ssm_im, sconv, cconv, ffn_buf, w, *, tt, nbt, nbt_odd):
    depth = w["g_mix"].shape[0]
    n_re, n_im, n_s, n_c, n_f = [], [], [], [], []
    for l in range(depth):
        i = l // 2
        if l % 2 == 0:
            h0 = _ssm_to_flat(ssm_re[i], ssm_im[i])
            s0 = jnp.transpose(sconv[i], (1, 0, 2))
            x, h1, s1 = _even_call(x, h0, s0, w, l, tt=tt, nbt=nbt)
            re, im = _flat_to_ssm(h1)
            n_re.append(re)
            n_im.append(im)
            n_s.append(jnp.transpose(s1, (1, 0, 2)))
        else:
            c0 = jnp.transpose(cconv[i], (1, 0, 2))
            x, c1 = _odd_call(x, c0, w, l, tt=tt, nbt=nbt_odd)
            n_c.append(jnp.transpose(c1, (1, 0, 2)))
        f0 = jnp.transpose(ffn_buf[l], (1, 0, 2))
        x, f1 = _ffn_call(x, f0, w, l, tt=tt, nbt=nbt, final_norm=(l == depth - 1))
        n_f.append(jnp.transpose(f1, (1, 0, 2)))
    return x, jnp.stack(n_re), jnp.stack(n_im), jnp.stack(n_s), jnp.stack(n_c), jnp.stack(n_f)


def kernel(x_prompt, x_sample, state_ssm_re, state_ssm_im, state_sconv, state_cconv, state_ffn, meta_tokens, norm_mix, norm_ffn, norm_final, w_in_even, ssm_lam_re, ssm_lam_im, ssm_log_dt, ssm_b_re, ssm_b_im, ssm_c_re, ssm_c_im, ssm_d, w_glu, b_glu, w_sconv, w_out_even, w_conf_pw1, b_conf_pw1, w_conf_dw, conf_ln_g, conf_ln_b, w_conf_pw2, b_conf_pw2, w_ffn_up, w_ffn_conv, w_ffn_down):
    w = _prep_params(dict(
        norm_mix=norm_mix, norm_ffn=norm_ffn, norm_final=norm_final, w_in_even=w_in_even,
        ssm_lam_re=ssm_lam_re, ssm_lam_im=ssm_lam_im, ssm_log_dt=ssm_log_dt, ssm_b_re=ssm_b_re,
        ssm_b_im=ssm_b_im, ssm_c_re=ssm_c_re, ssm_c_im=ssm_c_im, ssm_d=ssm_d, w_glu=w_glu,
        b_glu=b_glu, w_sconv=w_sconv, w_out_even=w_out_even, w_conf_pw1=w_conf_pw1,
        b_conf_pw1=b_conf_pw1, w_conf_dw=w_conf_dw, conf_ln_g=conf_ln_g, conf_ln_b=conf_ln_b,
        w_conf_pw2=w_conf_pw2, b_conf_pw2=b_conf_pw2, w_ffn_up=w_ffn_up, w_ffn_conv=w_ffn_conv,
        w_ffn_down=w_ffn_down))
    depth = norm_mix.shape[0]
    batch, seq, _ = x_prompt.shape
    dec_batch, dec_seq, _ = x_sample.shape
    dt = x_prompt.dtype
    n_even, n_odd = (depth + 1) // 2, depth // 2

    meta = jnp.broadcast_to(meta_tokens.astype(dt)[:, None, :], (N_META, batch, D_MODEL))
    xp = jnp.concatenate([meta, jnp.transpose(x_prompt, (1, 0, 2))], axis=0)
    zp_h = jnp.zeros((n_even, batch, N_SSM_GROUPS, SSM_STATE), dt)
    zp_s = jnp.zeros((n_even, batch, SCONV_W - 1, D_SCONV), dt)
    zp_c = jnp.zeros((n_odd, batch, CONF_W - 1, D_CONF), dt)
    zp_f = jnp.zeros((depth, batch, FFN_W - 1, D_FF), dt)
    yp, *p_states = _trunk(xp, zp_h, zp_h, zp_s, zp_c, zp_f, w,
                           tt=PROMPT_TILE_STEPS, nbt=batch, nbt_odd=batch)
    y_prompt = jnp.transpose(yp[N_META:], (1, 0, 2))

    xs = jnp.transpose(x_sample, (1, 0, 2))
    ys, *s_states = _trunk(xs, state_ssm_re, state_ssm_im, state_sconv, state_cconv, state_ffn, w,
                           tt=dec_seq, nbt=dec_batch, nbt_odd=SAMPLE_ODD_BATCH_TILE)
    y_sample = jnp.transpose(ys, (1, 0, 2))

    return (y_prompt, y_sample, *p_states, *s_states)
```

```python
import functools
import math

import jax
import jax.numpy as jnp
from jax import lax
from jax.experimental import pallas as pl
from jax.experimental.pallas import tpu as pltpu

D_MODEL = 1024
N_META = 16
SSM_GROUP = 16
SSM_STATE = 64
D_SSM = D_MODEL // 2
N_SSM_GROUPS = D_SSM // SSM_GROUP
D_SCONV = D_MODEL - D_SSM
SCONV_W = 3
D_IN_EVEN = D_SSM + 3 * D_SCONV
D_CONF = D_MODEL
CONF_W = 31
D_FF = ((8 * D_MODEL // 3 + 127) // 128) * 128
FFN_W = 3
EPS = 1e-6

LANES = 128
SUBLANES = 8
KB_COLS = LANES
N_KB = D_SSM // KB_COLS
KB_GROUPS = KB_COLS // SSM_GROUP
KB_STATES = KB_GROUPS * SSM_STATE
D_STATE = N_KB * 2 * KB_STATES
FFN_CHUNK = 256
VMEM_LIMIT_BYTES = 56 * 1024 * 1024
PROMPT_TILE_STEPS = 86
SAMPLE_ODD_BATCH_TILE = 32
CONV_BLOCK_TILES = 8

_F32 = jnp.float32
_BF16 = jnp.bfloat16


def _dot(a, b):
    return jnp.dot(a, b, preferred_element_type=_F32)


def _rms(x, g):
    return (x * lax.rsqrt(jnp.mean(x * x, axis=-1, keepdims=True) + EPS)) * g


def _sigmoid(x):
    return 1.0 / (1.0 + jnp.exp(-x))


def _gelu_tanh(x):
    c = math.sqrt(2.0 / math.pi)
    return x * (0.5 * (1.0 + jnp.tanh(c * (x + 0.044715 * (x * x * x)))))


def _shifted(carry, cur, nb, k, width):
    back = (width - 1 - k) * nb
    if back == 0:
        return cur
    tm = cur.shape[0]
    head = carry[carry.shape[0] - back:]
    if back >= tm:
        return head[:tm]
    return jnp.concatenate([head, cur[:tm - back]], axis=0)


def _short_conv(carry, cur, w, nb):
    width = w.shape[0]
    acc = None
    for k in range(width):
        term = w[k:k + 1, :] * _shifted(carry, cur, nb, k, width)
        acc = term if acc is None else acc + term
    tail = jnp.concatenate([carry, cur], axis=0)[cur.shape[0]:]
    return acc, tail


def _load_state(state_ref, rows_ref, nb):
    width = rows_ref.shape[1]
    for k in range(state_ref.shape[1] // width):
        rows_ref[k * nb:(k + 1) * nb, :] = state_ref[:, k * width:(k + 1) * width]


def _store_state(state_ref, rows_ref, row0, nb):
    width = rows_ref.shape[1]
    for k in range(state_ref.shape[1] // width):
        state_ref[:, k * width:(k + 1) * width] = rows_ref[row0 + k * nb:row0 + (k + 1) * nb, :]


def _pick_unroll(n):
    for u in (8, 4, 2):
        if n % u == 0:
            return u
    return 1


def _even_kernel(*refs, tt, nb, n_meta):
    if n_meta:
        x_ref, meta_ref, *refs = refs
    else:
        x_ref, *refs = refs
    (h0_ref, s0_ref, g_ref, win_ref, bcat_ref, lam_ref, ccat_ref, dskip_ref, wglu_ref, bglu_ref,
     wsc_ref, wout_ref, xo_ref, ho_ref, so_ref, hcar_ref, scar_ref, hb_ref) = refs
    t_idx = pl.program_id(1)
    n_t = pl.num_programs(1)
    tm = tt * nb

    @pl.when(t_idx == 0)
    def _():
        hcar_ref[...] = h0_ref[...]
        _load_state(s0_ref, scar_ref, nb)

    x = x_ref[...].reshape(tm, D_MODEL)
    if n_meta:
        meta = jnp.broadcast_to(meta_ref[...][:, None, :], (n_meta, nb, D_MODEL))
        first = jnp.concatenate([meta.reshape(n_meta * nb, D_MODEL), x[:tm - n_meta * nb]], axis=0)
        x = jnp.where(t_idx == 0, first, x)
    xn = _rms(x, g_ref[...]).astype(_BF16)
    z = _dot(xn, win_ref[...])
    u = z[:, :D_SSM]
    x_b = z[:, D_SSM:D_SSM + D_SCONV]
    g_b = z[:, D_SSM + D_SCONV:D_SSM + 2 * D_SCONV]
    g_c = z[:, D_SSM + 2 * D_SCONV:]
    ub = u.astype(_BF16)

    unroll = _pick_unroll(tt)
    y_parts = []
    for kb in range(N_KB):
        c0 = kb * 2 * KB_STATES
        hb_ref[...] = _dot(ub[:, kb * KB_COLS:(kb + 1) * KB_COLS], bcat_ref[kb])
        lam_re = jnp.broadcast_to(lam_ref[kb, 0:1, :], (SUBLANES, KB_STATES))
        lam_im = jnp.broadcast_to(lam_ref[kb, 1:2, :], (SUBLANES, KB_STATES))
        for j in range(nb // SUBLANES):
            rows = slice(j * SUBLANES, (j + 1) * SUBLANES)
            h_re0 = hcar_ref[rows, c0:c0 + KB_STATES]
            h_im0 = hcar_ref[rows, c0 + KB_STATES:c0 + 2 * KB_STATES]

            def step(t, carry, j=j, lam_re=lam_re, lam_im=lam_im):
                h_re, h_im = carry
                r = pl.ds(pl.multiple_of(t * nb + j * SUBLANES, SUBLANES), SUBLANES)
                n_re = lam_re * h_re - lam_im * h_im + hb_ref[r, 0:KB_STATES]
                n_im = lam_re * h_im + lam_im * h_re + hb_ref[r, KB_STATES:2 * KB_STATES]
                hb_ref[r, 0:KB_STATES] = n_re
                hb_ref[r, KB_STATES:2 * KB_STATES] = n_im
                return n_re, n_im

            h_re, h_im = lax.fori_loop(0, tt, step, (h_re0, h_im0), unroll=unroll)
            hcar_ref[rows, c0:c0 + KB_STATES] = h_re
            hcar_ref[rows, c0 + KB_STATES:c0 + 2 * KB_STATES] = h_im
        y_parts.append(_dot(hb_ref[...].astype(_BF16), ccat_ref[kb]))
    y = jnp.concatenate(y_parts, axis=1) + dskip_ref[...] * u

    y_a = _gelu_tanh(y)
    y_a = y_a * _sigmoid(_dot(y_a.astype(_BF16), wglu_ref[...]) + bglu_ref[...])

    conv, tail = _short_conv(scar_ref[...], g_c * x_b, wsc_ref[...], nb)
    scar_ref[...] = tail
    y_b = g_b * conv

    mixed = jnp.concatenate([y_a.astype(_BF16), y_b.astype(_BF16)], axis=1)
    xo_ref[...] = (x + _dot(mixed, wout_ref[...])).reshape(tt, nb, D_MODEL)

    @pl.when(t_idx == n_t - 1)
    def _():
        ho_ref[...] = hcar_ref[...]
        _store_state(so_ref, scar_ref, 0, nb)


def _odd_kernel(x_ref, c0_ref, g_ref, w1_ref, b1_ref, wdw_ref, lng_ref, lnb_ref, w2_ref, b2_ref,
                xo_ref, co_ref,
                win_ref, conv_ref, *, tt, nb, n_t):
    t_idx = pl.program_id(1)
    tm = tt * nb
    halo = (CONF_W - 1) * nb

    @pl.when(t_idx == 0)
    def _():
        _load_state(c0_ref, win_ref, nb)

    x = x_ref[...].reshape(tm, D_MODEL)
    xn = _rms(x, g_ref[...]).astype(_BF16)
    h = _dot(xn, w1_ref[...]) + b1_ref[...]
    win_ref[halo:halo + tm, :] = h[:, :D_CONF] * _sigmoid(h[:, D_CONF:])

    def conv_block(r0, n_tiles):
        view = win_ref.at[pl.ds(r0, halo + n_tiles * SUBLANES), :]
        for c in range(D_CONF // LANES):
            lanes = slice(c * LANES, (c + 1) * LANES)
            accs = [None] * n_tiles
            for k in range(CONF_W):
                tap = wdw_ref[k * SUBLANES:(k + 1) * SUBLANES, lanes]
                for j in range(n_tiles):
                    lo = k * nb + j * SUBLANES
                    term = tap * view[lo:lo + SUBLANES, lanes]
                    accs[j] = term if accs[j] is None else accs[j] + term
            for j in range(n_tiles):
                conv_ref[pl.ds(r0 + j * SUBLANES, SUBLANES), lanes] = accs[j]

    n_row_tiles = tm // SUBLANES
    block_rows = CONV_BLOCK_TILES * SUBLANES

    def conv_trip(i, _):
        conv_block(pl.multiple_of(i * block_rows, block_rows), CONV_BLOCK_TILES)
        return 0

    lax.fori_loop(0, n_row_tiles // CONV_BLOCK_TILES, conv_trip, 0)
    if n_row_tiles % CONV_BLOCK_TILES:
        conv_block(n_row_tiles // CONV_BLOCK_TILES * block_rows, n_row_tiles % CONV_BLOCK_TILES)

    conv = conv_ref[...]
    mu = jnp.mean(conv, axis=-1, keepdims=True)
    xc = conv - mu
    yv = xc * lax.rsqrt(jnp.mean(xc * xc, axis=-1, keepdims=True) + EPS)
    yv = yv * lng_ref[...] + lnb_ref[...]
    act = (yv * _sigmoid(yv)).astype(_BF16)

    out = _dot(act, w2_ref[...]) + b2_ref[...]
    xo_ref[...] = (x + out).reshape(tt, nb, D_MODEL)

    @pl.when(t_idx == n_t - 1)
    def _():
        _store_state(co_ref, win_ref, tm, nb)

    if n_t > 1:
        assert tt >= CONF_W - 1

        @pl.when(t_idx < n_t - 1)
        def _():
            win_ref[0:halo, :] = win_ref[tm:tm + halo, :]


def _ffn_kernel(x_ref, f0_ref, g_ref, wup_ref, wc_ref, wd_ref, gfin_ref,
                xo_ref, fo_ref,
                fcar_ref, acc_ref, *, tt, nb, final_norm):
    t_idx = pl.program_id(1)
    n_t = pl.num_programs(1)
    tm = tt * nb

    @pl.when(t_idx == 0)
    def _():
        _load_state(f0_ref, fcar_ref, nb)

    x = x_ref[...].reshape(tm, D_MODEL)
    xn = _rms(x, g_ref[...]).astype(_BF16)
    for c in range(D_FF // FFN_CHUNK):
        cols = slice(c * FFN_CHUNK, (c + 1) * FFN_CHUNK)
        a = _dot(xn, wup_ref[:, cols])
        gate = _dot(xn, wup_ref[:, D_FF + c * FFN_CHUNK:D_FF + (c + 1) * FFN_CHUNK])
        a_c, tail = _short_conv(fcar_ref[:, cols], a, wc_ref[:, cols], nb)
        fcar_ref[:, cols] = tail
        hidden = ((a_c * _sigmoid(a_c)) * gate).astype(_BF16)
        part = _dot(hidden, wd_ref[cols, :])
        if c == 0:
            acc_ref[...] = part
        else:
            acc_ref[...] += part
    y = x + acc_ref[...]
    if final_norm:
        y = _rms(y, gfin_ref[...])
    xo_ref[...] = y.reshape(tt, nb, D_MODEL)

    @pl.when(t_idx == n_t - 1)
    def _():
        _store_state(fo_ref, fcar_ref, 0, nb)


def _layer_spec(arr, layer):
    tail = (0,) * (arr.ndim - 1)
    return pl.BlockSpec((None,) + arr.shape[1:], lambda b, t: (layer,) + tail,
                        pipeline_mode=pl.Buffered(1))


def _x_spec(tt, nbt):
    return pl.BlockSpec((tt, nbt, D_MODEL), lambda b, t: (t, b, 0))


def _state_spec(steps, nbt, width):
    return pl.BlockSpec((nbt, steps * width), lambda b, t: (b, 0))


_PARAMS = pltpu.CompilerParams(dimension_semantics=("arbitrary", "arbitrary"),
                               vmem_limit_bytes=VMEM_LIMIT_BYTES)

_EVEN_KEYS = ("w_in", "bcat", "lam", "ccat", "dskip", "w_glu", "b_glu", "w_sconv", "w_out")
_ODD_KEYS = ("w1", "b1", "w_dw", "ln_g", "ln_b", "w2", "b2")
_FFN_KEYS = ("g_ffn", "w_up", "w_conv", "w_down")


def _even_call(x, h0, s0, w, layer, *, tt, nbt, meta=None):
    n_meta = 0 if meta is None else meta.shape[0]
    length, nb = x.shape[0] + n_meta, x.shape[1]
    grid = (nb // nbt, length // tt)
    w_specs = [_layer_spec(w["g_mix"], layer)] + [_layer_spec(w[k], layer // 2) for k in _EVEN_KEYS]
    if n_meta:
        x_specs = [pl.BlockSpec((pl.Element(tt), pl.Element(nbt), pl.Element(D_MODEL)),
                                lambda b, t: (jnp.maximum(t * tt - n_meta, 0), b * nbt, 0)),
                   pl.BlockSpec(meta.shape, lambda b, t: (0, 0))]
        x_args = (x, meta)
    else:
        x_specs, x_args = [_x_spec(tt, nbt)], (x,)
    return pl.pallas_call(
        functools.partial(_even_kernel, tt=tt, nb=nbt, n_meta=n_meta),
        grid=grid,
        in_specs=x_specs + [pl.BlockSpec((nbt, D_STATE), lambda b, t: (b, 0)),
                            _state_spec(SCONV_W - 1, nbt, D_SCONV)] + w_specs,
        out_specs=[_x_spec(tt, nbt),
                   pl.BlockSpec((nbt, D_STATE), lambda b, t: (b, 0)),
                   _state_spec(SCONV_W - 1, nbt, D_SCONV)],
        out_shape=[jax.ShapeDtypeStruct((length, nb, D_MODEL), _F32),
                   jax.ShapeDtypeStruct(h0.shape, _F32),
                   jax.ShapeDtypeStruct(s0.shape, _F32)],
        scratch_shapes=[pltpu.VMEM((nbt, D_STATE), _F32),
                        pltpu.VMEM(((SCONV_W - 1) * nbt, D_SCONV), _F32),
                        pltpu.VMEM((tt * nbt, 2 * KB_STATES), _F32)],
        compiler_params=_PARAMS,
        name="even_mixer",
    )(*x_args, h0, s0, w["g_mix"], *[w[k] for k in _EVEN_KEYS])


def _odd_call(x, c0, w, layer, *, tt, nbt):
    length, nb, _ = x.shape
    n_t = length // tt
    grid = (nb // nbt, n_t)
    w_specs = [_layer_spec(w["g_mix"], layer)] + [_layer_spec(w[k], layer // 2) for k in _ODD_KEYS]
    return pl.pallas_call(
        functools.partial(_odd_kernel, tt=tt, nb=nbt, n_t=n_t),
        grid=grid,
        in_specs=[_x_spec(tt, nbt), _state_spec(CONF_W - 1, nbt, D_CONF)] + w_specs,
        out_specs=[_x_spec(tt, nbt), _state_spec(CONF_W - 1, nbt, D_CONF)],
        out_shape=[jax.ShapeDtypeStruct(x.shape, _F32), jax.ShapeDtypeStruct(c0.shape, _F32)],
        scratch_shapes=[pltpu.VMEM(((CONF_W - 1 + tt) * nbt, D_CONF), _F32),
                        pltpu.VMEM((tt * nbt, D_CONF), _F32)],
        compiler_params=_PARAMS,
        name="odd_mixer",
    )(x, c0, w["g_mix"], *[w[k] for k in _ODD_KEYS])


def _ffn_call(x, f0, w, layer, *, tt, nbt, final_norm):
    length, nb, _ = x.shape
    grid = (nb // nbt, length // tt)
    w_specs = [_layer_spec(w[k], layer) for k in _FFN_KEYS] + [_layer_spec(w["g_final"], 0)]
    return pl.pallas_call(
        functools.partial(_ffn_kernel, tt=tt, nb=nbt, final_norm=final_norm),
        grid=grid,
        in_specs=[_x_spec(tt, nbt), _state_spec(FFN_W - 1, nbt, D_FF)] + w_specs,
        out_specs=[_x_spec(tt, nbt), _state_spec(FFN_W - 1, nbt, D_FF)],
        out_shape=[jax.ShapeDtypeStruct(x.shape, _F32), jax.ShapeDtypeStruct(f0.shape, _F32)],
        scratch_shapes=[pltpu.VMEM(((FFN_W - 1) * nbt, D_FF), _F32),
                        pltpu.VMEM((tt * nbt, D_MODEL), _F32)],
        compiler_params=_PARAMS,
        name="conv_ffn",
    )(x, f0, *[w[k] for k in _FFN_KEYS], w["g_final"])


def _rows(v):
    return v.reshape(v.shape[0], 1, v.shape[-1]).astype(_F32)


def _prep_params(p):
    n_even = p["w_in_even"].shape[0]
    p_, s_ = SSM_STATE, SSM_GROUP
    lam = lax.complex(p["ssm_lam_re"].astype(_F32), p["ssm_lam_im"].astype(_F32))
    dt = jnp.exp(p["ssm_log_dt"].astype(_F32))[..., None]
    lam_bar = jnp.exp(lam * dt)
    b_bar = ((lam_bar - 1.0) / lam)[..., None] * lax.complex(
        p["ssm_b_re"].astype(_F32), p["ssm_b_im"].astype(_F32))
    eye = jnp.eye(KB_GROUPS, dtype=_F32)

    def b_block(v):
        v = v.reshape(n_even, N_KB, KB_GROUPS, p_, s_)
        return jnp.einsum("ekgpi,gh->ekgihp", v, eye).reshape(n_even, N_KB, KB_COLS, KB_STATES)

    def c_block(v):
        v = v.reshape(n_even, N_KB, KB_GROUPS, s_, p_)
        return jnp.einsum("ekgip,gh->ekgphi", v, eye).reshape(n_even, N_KB, KB_STATES, KB_COLS)

    bcat = jnp.concatenate([b_block(jnp.real(b_bar)), b_block(jnp.imag(b_bar))], axis=3)
    ccat = jnp.concatenate([c_block(p["ssm_c_re"].astype(_F32)),
                            -c_block(p["ssm_c_im"].astype(_F32))], axis=2)
    lamcat = jnp.stack([jnp.real(lam_bar).reshape(n_even, N_KB, KB_STATES),
                        jnp.imag(lam_bar).reshape(n_even, N_KB, KB_STATES)], axis=2)
    return dict(
        g_mix=_rows(p["norm_mix"]), g_ffn=_rows(p["norm_ffn"]), g_final=_rows(p["norm_final"][None]),
        w_in=p["w_in_even"].astype(_BF16), bcat=bcat.astype(_BF16), lam=lamcat,
        ccat=ccat.astype(_BF16), dskip=_rows(p["ssm_d"]), w_glu=p["w_glu"].astype(_BF16),
        b_glu=_rows(p["b_glu"]), w_sconv=p["w_sconv"].astype(_F32),
        w_out=p["w_out_even"].astype(_BF16),
        w1=p["w_conf_pw1"].astype(_BF16), b1=_rows(p["b_conf_pw1"]),
        w_dw=jnp.repeat(p["w_conf_dw"].astype(_F32), SUBLANES, axis=1),
        ln_g=_rows(p["conf_ln_g"]), ln_b=_rows(p["conf_ln_b"]),
        w2=p["w_conf_pw2"].astype(_BF16), b2=_rows(p["b_conf_pw2"]),
        w_up=p["w_ffn_up"].astype(_BF16), w_conv=p["w_ffn_conv"].astype(_F32),
        w_down=p["w_ffn_down"].astype(_BF16))


def _flat_steps(v):
    return v.reshape(v.shape[0], -1)


def _ssm_to_flat(re, im):
    nb = re.shape[0]
    re = re.reshape(nb, N_KB, KB_STATES)
    im = im.reshape(nb, N_KB, KB_STATES)
    return jnp.concatenate([re, im], axis=2).reshape(nb, D_STATE).astype(_F32)


def _flat_to_ssm(h):
    nb = h.shape[0]
    h = h.reshape(nb, N_KB, 2, KB_STATES)
    shape = (nb, N_SSM_GROUPS, SSM_STATE)
    return h[:, :, 0].reshape(shape), h[:, :, 1].reshape(shape)


def _trunk(x, ssm_re, ssm_im, sconv, cconv, ffn_buf, w, *, tt, nbt, nbt_odd, meta=None):
    depth = w["g_mix"].shape[0]
    n_re, n_im, n_s, n_c, n_f = [], [], [], [], []
    for l in range(depth):
        i = l // 2
        if l % 2 == 0:
            h0 = _ssm_to_flat(ssm_re[i], ssm_im[i])
            x, h1, s1 = _even_call(x, h0, _flat_steps(sconv[i]), w, l, tt=tt, nbt=nbt,
                                   meta=meta if l == 0 else None)
            re, im = _flat_to_ssm(h1)
            n_re.append(re)
            n_im.append(im)
            n_s.append(s1.reshape(sconv[i].shape))
        else:
            x, c1 = _odd_call(x, _flat_steps(cconv[i]), w, l, tt=tt, nbt=nbt_odd)
            n_c.append(c1.reshape(cconv[i].shape))
        x, f1 = _ffn_call(x, _flat_steps(ffn_buf[l]), w, l, tt=tt, nbt=nbt, final_norm=(l == depth - 1))
        n_f.append(f1.reshape(ffn_buf[l].shape))
    return x, jnp.stack(n_re), jnp.stack(n_im), jnp.stack(n_s), jnp.stack(n_c), jnp.stack(n_f)


def kernel(x_prompt, x_sample, state_ssm_re, state_ssm_im, state_sconv, state_cconv, state_ffn, meta_tokens, norm_mix, norm_ffn, norm_final, w_in_even, ssm_lam_re, ssm_lam_im, ssm_log_dt, ssm_b_re, ssm_b_im, ssm_c_re, ssm_c_im, ssm_d, w_glu, b_glu, w_sconv, w_out_even, w_conf_pw1, b_conf_pw1, w_conf_dw, conf_ln_g, conf_ln_b, w_conf_pw2, b_conf_pw2, w_ffn_up, w_ffn_conv, w_ffn_down):
    w = _prep_params(dict(
        norm_mix=norm_mix, norm_ffn=norm_ffn, norm_final=norm_final, w_in_even=w_in_even,
        ssm_lam_re=ssm_lam_re, ssm_lam_im=ssm_lam_im, ssm_log_dt=ssm_log_dt, ssm_b_re=ssm_b_re,
        ssm_b_im=ssm_b_im, ssm_c_re=ssm_c_re, ssm_c_im=ssm_c_im, ssm_d=ssm_d, w_glu=w_glu,
        b_glu=b_glu, w_sconv=w_sconv, w_out_even=w_out_even, w_conf_pw1=w_conf_pw1,
        b_conf_pw1=b_conf_pw1, w_conf_dw=w_conf_dw, conf_ln_g=conf_ln_g, conf_ln_b=conf_ln_b,
        w_conf_pw2=w_conf_pw2, b_conf_pw2=b_conf_pw2, w_ffn_up=w_ffn_up, w_ffn_conv=w_ffn_conv,
        w_ffn_down=w_ffn_down))
    depth = norm_mix.shape[0]
    batch, seq, _ = x_prompt.shape
    dec_batch, dec_seq, _ = x_sample.shape
    dt = x_prompt.dtype
    n_even, n_odd = (depth + 1) // 2, depth // 2

    xp = jnp.transpose(x_prompt, (1, 0, 2))
    zp_h = jnp.zeros((n_even, batch, N_SSM_GROUPS, SSM_STATE), dt)
    zp_s = jnp.zeros((n_even, batch, SCONV_W - 1, D_SCONV), dt)
    zp_c = jnp.zeros((n_odd, batch, CONF_W - 1, D_CONF), dt)
    zp_f = jnp.zeros((depth, batch, FFN_W - 1, D_FF), dt)
    yp, *p_states = _trunk(xp, zp_h, zp_h, zp_s, zp_c, zp_f, w,
                           tt=PROMPT_TILE_STEPS, nbt=batch, nbt_odd=batch,
                           meta=meta_tokens.astype(dt))
    y_prompt = jnp.transpose(yp[N_META:], (1, 0, 2))

    xs = jnp.transpose(x_sample, (1, 0, 2))
    ys, *s_states = _trunk(xs, state_ssm_re, state_ssm_im, state_sconv, state_cconv, state_ffn, w,
                           tt=dec_seq, nbt=dec_batch, nbt_odd=SAMPLE_ODD_BATCH_TILE)
    y_sample = jnp.transpose(ys, (1, 0, 2))

    return (y_prompt, y_sample, *p_states, *s_states)
```

```python
import functools
import math

import jax
import jax.numpy as jnp
from jax import lax
from jax.experimental import pallas as pl
from jax.experimental.pallas import tpu as pltpu

D_MODEL = 1024
N_META = 16
SSM_GROUP = 16
SSM_STATE = 64
D_SSM = D_MODEL // 2
N_SSM_GROUPS = D_SSM // SSM_GROUP
D_SCONV = D_MODEL - D_SSM
SCONV_W = 3
D_IN_EVEN = D_SSM + 3 * D_SCONV
D_CONF = D_MODEL
CONF_W = 31
D_FF = ((8 * D_MODEL // 3 + 127) // 128) * 128
FFN_W = 3
EPS = 1e-6

LANES = 128
SUBLANES = 8
KB_COLS = LANES
N_KB = D_SSM // KB_COLS
KB_GROUPS = KB_COLS // SSM_GROUP
KB_STATES = KB_GROUPS * SSM_STATE
D_STATE = N_KB * 2 * KB_STATES
FFN_CHUNK = 256
VMEM_LIMIT_BYTES = 56 * 1024 * 1024
PROMPT_TILE_STEPS = 86
SAMPLE_ODD_BATCH_TILE = 32
CONV_BLOCK_TILES = 8

_F32 = jnp.float32
_BF16 = jnp.bfloat16


def _dot(a, b):
    return jnp.dot(a, b, preferred_element_type=_F32)


def _rms(x, g):
    return (x * lax.rsqrt(jnp.mean(x * x, axis=-1, keepdims=True) + EPS)) * g


def _sigmoid(x):
    return 1.0 / (1.0 + jnp.exp(-x))


def _gelu_tanh(x):
    c = math.sqrt(2.0 / math.pi)
    return x * (0.5 * (1.0 + jnp.tanh(c * (x + 0.044715 * (x * x * x)))))


def _shifted(carry, cur, nb, k, width):
    back = (width - 1 - k) * nb
    if back == 0:
        return cur
    tm = cur.shape[0]
    head = carry[carry.shape[0] - back:]
    if back >= tm:
        return head[:tm]
    return jnp.concatenate([head, cur[:tm - back]], axis=0)


def _short_conv(carry, cur, w, nb):
    width = w.shape[0]
    acc = None
    for k in range(width):
        term = w[k:k + 1, :] * _shifted(carry, cur, nb, k, width)
        acc = term if acc is None else acc + term
    tail = jnp.concatenate([carry, cur], axis=0)[cur.shape[0]:]
    return acc, tail


def _even_kernel(*refs, tt, nb, n_meta):
    if n_meta:
        x_ref, meta_ref, *refs = refs
    else:
        x_ref, *refs = refs
    (h0_ref, s0_ref, g_ref, win_ref, bcat_ref, lam_ref, ccat_ref, dskip_ref, wglu_ref, bglu_ref,
     wsc_ref, wout_ref, xo_ref, ho_ref, so_ref, hcar_ref, scar_ref, hb_ref) = refs
    t_idx = pl.program_id(1)
    n_t = pl.num_programs(1)
    tm = tt * nb

    @pl.when(t_idx == 0)
    def _():
        hcar_ref[...] = h0_ref[...]
        scar_ref[...] = s0_ref[...].reshape(2 * nb, D_SCONV)

    x = x_ref[...].reshape(tm, D_MODEL)
    if n_meta:
        meta = jnp.broadcast_to(meta_ref[...][:, None, :], (n_meta, nb, D_MODEL))
        first = jnp.concatenate([meta.reshape(n_meta * nb, D_MODEL), x[:tm - n_meta * nb]], axis=0)
        x = jnp.where(t_idx == 0, first, x)
    xn = _rms(x, g_ref[...]).astype(_BF16)
    z = _dot(xn, win_ref[...])
    u = z[:, :D_SSM]
    x_b = z[:, D_SSM:D_SSM + D_SCONV]
    g_b = z[:, D_SSM + D_SCONV:D_SSM + 2 * D_SCONV]
    g_c = z[:, D_SSM + 2 * D_SCONV:]
    ub = u.astype(_BF16)

    y_parts = []
    for kb in range(N_KB):
        c0 = kb * 2 * KB_STATES
        hb_ref[...] = _dot(ub[:, kb * KB_COLS:(kb + 1) * KB_COLS], bcat_ref[kb])
        lam_re = jnp.broadcast_to(lam_ref[kb, 0:1, :], (SUBLANES, KB_STATES))
        lam_im = jnp.broadcast_to(lam_ref[kb, 1:2, :], (SUBLANES, KB_STATES))
        for j in range(nb // SUBLANES):
            rows = slice(j * SUBLANES, (j + 1) * SUBLANES)
            h_re0 = hcar_ref[rows, c0:c0 + KB_STATES]
            h_im0 = hcar_ref[rows, c0 + KB_STATES:c0 + 2 * KB_STATES]

            def step(t, carry, j=j, lam_re=lam_re, lam_im=lam_im):
                h_re, h_im = carry
                r = pl.ds(t * nb + j * SUBLANES, SUBLANES)
                n_re = lam_re * h_re - lam_im * h_im + hb_ref[r, 0:KB_STATES]
                n_im = lam_re * h_im + lam_im * h_re + hb_ref[r, KB_STATES:2 * KB_STATES]
                hb_ref[r, 0:KB_STATES] = n_re
                hb_ref[r, KB_STATES:2 * KB_STATES] = n_im
                return n_re, n_im

            h_re, h_im = h_re0, h_im0
            for t in range(tt):
                h_re, h_im = step(t, (h_re, h_im))
            hcar_ref[rows, c0:c0 + KB_STATES] = h_re
            hcar_ref[rows, c0 + KB_STATES:c0 + 2 * KB_STATES] = h_im
        y_parts.append(_dot(hb_ref[...].astype(_BF16), ccat_ref[kb]))
    y = jnp.concatenate(y_parts, axis=1) + dskip_ref[...] * u

    y_a = _gelu_tanh(y)
    y_a = y_a * _sigmoid(_dot(y_a.astype(_BF16), wglu_ref[...]) + bglu_ref[...])

    conv, tail = _short_conv(scar_ref[...], g_c * x_b, wsc_ref[...], nb)
    scar_ref[...] = tail
    y_b = g_b * conv

    mixed = jnp.concatenate([y_a.astype(_BF16), y_b.astype(_BF16)], axis=1)
    xo_ref[...] = (x + _dot(mixed, wout_ref[...])).reshape(tt, nb, D_MODEL)

    @pl.when(t_idx == n_t - 1)
    def _():
        ho_ref[...] = hcar_ref[...]
        so_ref[...] = scar_ref[...].reshape(2, nb, D_SCONV)


def _odd_kernel(x_ref, c0_ref, g_ref, w1_ref, b1_ref, wdw_ref, lng_ref, lnb_ref, w2_ref, b2_ref,
                xo_ref, co_ref,
                win_ref, conv_ref, *, tt, nb, n_t):
    t_idx = pl.program_id(1)
    tm = tt * nb
    halo = (CONF_W - 1) * nb

    @pl.when(t_idx == 0)
    def _():
        win_ref[0:halo, :] = c0_ref[...].reshape(halo, D_CONF)

    x = x_ref[...].reshape(tm, D_MODEL)
    xn = _rms(x, g_ref[...]).astype(_BF16)
    h = _dot(xn, w1_ref[...]) + b1_ref[...]
    win_ref[halo:halo + tm, :] = h[:, :D_CONF] * _sigmoid(h[:, D_CONF:])

    def conv_block(r0, n_tiles):
        view = win_ref.at[pl.ds(r0, halo + n_tiles * SUBLANES), :]
        for c in range(D_CONF // LANES):
            lanes = slice(c * LANES, (c + 1) * LANES)
            accs = [None] * n_tiles
            for k in range(CONF_W):
                tap = wdw_ref[k * SUBLANES:(k + 1) * SUBLANES, lanes]
                for j in range(n_tiles):
                    lo = k * nb + j * SUBLANES
                    term = tap * view[lo:lo + SUBLANES, lanes]
                    accs[j] = term if accs[j] is None else accs[j] + term
            for j in range(n_tiles):
                conv_ref[pl.ds(r0 + j * SUBLANES, SUBLANES), lanes] = accs[j]

    n_row_tiles = tm // SUBLANES
    block_rows = CONV_BLOCK_TILES * SUBLANES

    def conv_trip(i, _):
        conv_block(pl.multiple_of(i * block_rows, block_rows), CONV_BLOCK_TILES)
        return 0

    lax.fori_loop(0, n_row_tiles // CONV_BLOCK_TILES, conv_trip, 0)
    if n_row_tiles % CONV_BLOCK_TILES:
        conv_block(n_row_tiles // CONV_BLOCK_TILES * block_rows, n_row_tiles % CONV_BLOCK_TILES)

    conv = conv_ref[...]
    mu = jnp.mean(conv, axis=-1, keepdims=True)
    xc = conv - mu
    yv = xc * lax.rsqrt(jnp.mean(xc * xc, axis=-1, keepdims=True) + EPS)
    yv = yv * lng_ref[...] + lnb_ref[...]
    act = (yv * _sigmoid(yv)).astype(_BF16)

    out = _dot(act, w2_ref[...]) + b2_ref[...]
    xo_ref[...] = (x + out).reshape(tt, nb, D_MODEL)

    @pl.when(t_idx == n_t - 1)
    def _():
        co_ref[...] = win_ref[tm:tm + halo, :].reshape(CONF_W - 1, nb, D_CONF)

    if n_t > 1:
        assert tt >= CONF_W - 1

        @pl.when(t_idx < n_t - 1)
        def _():
            win_ref[0:halo, :] = win_ref[tm:tm + halo, :]


def _ffn_kernel(x_ref, f0_ref, g_ref, wup_ref, wc_ref, wd_ref, gfin_ref,
                xo_ref, fo_ref,
                fcar_ref, acc_ref, *, tt, nb, final_norm):
    t_idx = pl.program_id(1)
    n_t = pl.num_programs(1)
    tm = tt * nb

    @pl.when(t_idx == 0)
    def _():
        fcar_ref[...] = f0_ref[...].reshape(2 * nb, D_FF)

    x = x_ref[...].reshape(tm, D_MODEL)
    xn = _rms(x, g_ref[...]).astype(_BF16)
    for c in range(D_FF // FFN_CHUNK):
        cols = slice(c * FFN_CHUNK, (c + 1) * FFN_CHUNK)
        a = _dot(xn, wup_ref[:, cols])
        gate = _dot(xn, wup_ref[:, D_FF + c * FFN_CHUNK:D_FF + (c + 1) * FFN_CHUNK])
        a_c, tail = _short_conv(fcar_ref[:, cols], a, wc_ref[:, cols], nb)
        fcar_ref[:, cols] = tail
        hidden = ((a_c * _sigmoid(a_c)) * gate).astype(_BF16)
        part = _dot(hidden, wd_ref[cols, :])
        if c == 0:
            acc_ref[...] = part
        else:
            acc_ref[...] += part
    y = x + acc_ref[...]
    if final_norm:
        y = _rms(y, gfin_ref[...])
    xo_ref[...] = y.reshape(tt, nb, D_MODEL)

    @pl.when(t_idx == n_t - 1)
    def _():
        fo_ref[...] = fcar_ref[...].reshape(2, nb, D_FF)


def _layer_spec(arr, layer):
    tail = (0,) * (arr.ndim - 1)
    return pl.BlockSpec((None,) + arr.shape[1:], lambda b, t: (layer,) + tail,
                        pipeline_mode=pl.Buffered(1))


def _x_spec(tt, nbt):
    return pl.BlockSpec((tt, nbt, D_MODEL), lambda b, t: (t, b, 0))


def _state_spec(steps, nbt, width):
    return pl.BlockSpec((steps, nbt, width), lambda b, t: (0, b, 0))


_PARAMS = pltpu.CompilerParams(dimension_semantics=("arbitrary", "arbitrary"),
                               vmem_limit_bytes=VMEM_LIMIT_BYTES)

_EVEN_KEYS = ("w_in", "bcat", "lam", "ccat", "dskip", "w_glu", "b_glu", "w_sconv", "w_out")
_ODD_KEYS = ("w1", "b1", "w_dw", "ln_g", "ln_b", "w2", "b2")
_FFN_KEYS = ("g_ffn", "w_up", "w_conv", "w_down")


def _even_call(x, h0, s0, w, layer, *, tt, nbt, meta=None):
    n_meta = 0 if meta is None else meta.shape[0]
    length, nb = x.shape[0] + n_meta, x.shape[1]
    grid = (nb // nbt, length // tt)
    w_specs = [_layer_spec(w["g_mix"], layer)] + [_layer_spec(w[k], layer // 2) for k in _EVEN_KEYS]
    if n_meta:
        x_specs = [pl.BlockSpec((pl.Element(tt), pl.Element(nbt), pl.Element(D_MODEL)),
                                lambda b, t: (jnp.maximum(t * tt - n_meta, 0), b * nbt, 0)),
                   pl.BlockSpec(meta.shape, lambda b, t: (0, 0))]
        x_args = (x, meta)
    else:
        x_specs, x_args = [_x_spec(tt, nbt)], (x,)
    return pl.pallas_call(
        functools.partial(_even_kernel, tt=tt, nb=nbt, n_meta=n_meta),
        grid=grid,
        in_specs=x_specs + [pl.BlockSpec((nbt, D_STATE), lambda b, t: (b, 0)),
                            _state_spec(SCONV_W - 1, nbt, D_SCONV)] + w_specs,
        out_specs=[_x_spec(tt, nbt),
                   pl.BlockSpec((nbt, D_STATE), lambda b, t: (b, 0)),
                   _state_spec(SCONV_W - 1, nbt, D_SCONV)],
        out_shape=[jax.ShapeDtypeStruct((length, nb, D_MODEL), _F32),
                   jax.ShapeDtypeStruct(h0.shape, _F32),
                   jax.ShapeDtypeStruct(s0.shape, _F32)],
        scratch_shapes=[pltpu.VMEM((nbt, D_STATE), _F32),
                        pltpu.VMEM(((SCONV_W - 1) * nbt, D_SCONV), _F32),
                        pltpu.VMEM((tt * nbt, 2 * KB_STATES), _F32)],
        compiler_params=_PARAMS,
        name="even_mixer",
    )(*x_args, h0, s0, w["g_mix"], *[w[k] for k in _EVEN_KEYS])


def _odd_call(x, c0, w, layer, *, tt, nbt):
    length, nb, _ = x.shape
    n_t = length // tt
    grid = (nb // nbt, n_t)
    w_specs = [_layer_spec(w["g_mix"], layer)] + [_layer_spec(w[k], layer // 2) for k in _ODD_KEYS]
    return pl.pallas_call(
        functools.partial(_odd_kernel, tt=tt, nb=nbt, n_t=n_t),
        grid=grid,
        in_specs=[_x_spec(tt, nbt), _state_spec(CONF_W - 1, nbt, D_CONF)] + w_specs,
        out_specs=[_x_spec(tt, nbt), _state_spec(CONF_W - 1, nbt, D_CONF)],
        out_shape=[jax.ShapeDtypeStruct(x.shape, _F32), jax.ShapeDtypeStruct(c0.shape, _F32)],
        scratch_shapes=[pltpu.VMEM(((CONF_W - 1 + tt) * nbt, D_CONF), _F32),
                        pltpu.VMEM((tt * nbt, D_CONF), _F32)],
        compiler_params=_PARAMS,
        name="odd_mixer",
    )(x, c0, w["g_mix"], *[w[k] for k in _ODD_KEYS])


def _ffn_call(x, f0, w, layer, *, tt, nbt, final_norm):
    length, nb, _ = x.shape
    grid = (nb // nbt, length // tt)
    w_specs = [_layer_spec(w[k], layer) for k in _FFN_KEYS] + [_layer_spec(w["g_final"], 0)]
    return pl.pallas_call(
        functools.partial(_ffn_kernel, tt=tt, nb=nbt, final_norm=final_norm),
        grid=grid,
        in_specs=[_x_spec(tt, nbt), _state_spec(FFN_W - 1, nbt, D_FF)] + w_specs,
        out_specs=[_x_spec(tt, nbt), _state_spec(FFN_W - 1, nbt, D_FF)],
        out_shape=[jax.ShapeDtypeStruct(x.shape, _F32), jax.ShapeDtypeStruct(f0.shape, _F32)],
        scratch_shapes=[pltpu.VMEM(((FFN_W - 1) * nbt, D_FF), _F32),
                        pltpu.VMEM((tt * nbt, D_MODEL), _F32)],
        compiler_params=_PARAMS,
        name="conv_ffn",
    )(x, f0, *[w[k] for k in _FFN_KEYS], w["g_final"])


def _rows(v):
    return v.reshape(v.shape[0], 1, v.shape[-1]).astype(_F32)


def _prep_params(p):
    n_even = p["w_in_even"].shape[0]
    p_, s_ = SSM_STATE, SSM_GROUP
    lam = lax.complex(p["ssm_lam_re"].astype(_F32), p["ssm_lam_im"].astype(_F32))
    dt = jnp.exp(p["ssm_log_dt"].astype(_F32))[..., None]
    lam_bar = jnp.exp(lam * dt)
    b_bar = ((lam_bar - 1.0) / lam)[..., None] * lax.complex(
        p["ssm_b_re"].astype(_F32), p["ssm_b_im"].astype(_F32))
    eye = jnp.eye(KB_GROUPS, dtype=_F32)

    def b_block(v):
        v = v.reshape(n_even, N_KB, KB_GROUPS, p_, s_)
        return jnp.einsum("ekgpi,gh->ekgihp", v, eye).reshape(n_even, N_KB, KB_COLS, KB_STATES)

    def c_block(v):
        v = v.reshape(n_even, N_KB, KB_GROUPS, s_, p_)
        return jnp.einsum("ekgip,gh->ekgphi", v, eye).reshape(n_even, N_KB, KB_STATES, KB_COLS)

    bcat = jnp.concatenate([b_block(jnp.real(b_bar)), b_block(jnp.imag(b_bar))], axis=3)
    ccat = jnp.concatenate([c_block(p["ssm_c_re"].astype(_F32)),
                            -c_block(p["ssm_c_im"].astype(_F32))], axis=2)
    lamcat = jnp.stack([jnp.real(lam_bar).reshape(n_even, N_KB, KB_STATES),
                        jnp.imag(lam_bar).reshape(n_even, N_KB, KB_STATES)], axis=2)
    return dict(
        g_mix=_rows(p["norm_mix"]), g_ffn=_rows(p["norm_ffn"]), g_final=_rows(p["norm_final"][None]),
        w_in=p["w_in_even"].astype(_BF16), bcat=bcat.astype(_BF16), lam=lamcat,
        ccat=ccat.astype(_BF16), dskip=_rows(p["ssm_d"]), w_glu=p["w_glu"].astype(_BF16),
        b_glu=_rows(p["b_glu"]), w_sconv=p["w_sconv"].astype(_F32),
        w_out=p["w_out_even"].astype(_BF16),
        w1=p["w_conf_pw1"].astype(_BF16), b1=_rows(p["b_conf_pw1"]),
        w_dw=jnp.repeat(p["w_conf_dw"].astype(_F32), SUBLANES, axis=1),
        ln_g=_rows(p["conf_ln_g"]), ln_b=_rows(p["conf_ln_b"]),
        w2=p["w_conf_pw2"].astype(_BF16), b2=_rows(p["b_conf_pw2"]),
        w_up=p["w_ffn_up"].astype(_BF16), w_conv=p["w_ffn_conv"].astype(_F32),
        w_down=p["w_ffn_down"].astype(_BF16))


def _time_major(v):
    return jnp.transpose(v, (1, 0, 2))


def _ssm_to_flat(re, im):
    nb = re.shape[0]
    re = re.reshape(nb, N_KB, KB_STATES)
    im = im.reshape(nb, N_KB, KB_STATES)
    return jnp.concatenate([re, im], axis=2).reshape(nb, D_STATE).astype(_F32)


def _flat_to_ssm(h):
    nb = h.shape[0]
    h = h.reshape(nb, N_KB, 2, KB_STATES)
    shape = (nb, N_SSM_GROUPS, SSM_STATE)
    return h[:, :, 0].reshape(shape), h[:, :, 1].reshape(shape)


def _trunk(x, ssm_re, ssm_im, sconv, cconv, ffn_buf, w, *, tt, nbt, nbt_odd, meta=None):
    depth = w["g_mix"].shape[0]
    n_re, n_im, n_s, n_c, n_f = [], [], [], [], []
    for l in range(depth):
        i = l // 2
        if l % 2 == 0:
            h0 = _ssm_to_flat(ssm_re[i], ssm_im[i])
            x, h1, s1 = _even_call(x, h0, _time_major(sconv[i]), w, l, tt=tt, nbt=nbt,
                                   meta=meta if l == 0 else None)
            re, im = _flat_to_ssm(h1)
            n_re.append(re)
            n_im.append(im)
            n_s.append(_time_major(s1))
        else:
            x, c1 = _odd_call(x, _time_major(cconv[i]), w, l, tt=tt, nbt=nbt_odd)
            n_c.append(_time_major(c1))
        x, f1 = _ffn_call(x, _time_major(ffn_buf[l]), w, l, tt=tt, nbt=nbt, final_norm=(l == depth - 1))
        n_f.append(_time_major(f1))
    return x, jnp.stack(n_re), jnp.stack(n_im), jnp.stack(n_s), jnp.stack(n_c), jnp.stack(n_f)


def kernel(x_prompt, x_sample, state_ssm_re, state_ssm_im, state_sconv, state_cconv, state_ffn, meta_tokens, norm_mix, norm_ffn, norm_final, w_in_even, ssm_lam_re, ssm_lam_im, ssm_log_dt, ssm_b_re, ssm_b_im, ssm_c_re, ssm_c_im, ssm_d, w_glu, b_glu, w_sconv, w_out_even, w_conf_pw1, b_conf_pw1, w_conf_dw, conf_ln_g, conf_ln_b, w_conf_pw2, b_conf_pw2, w_ffn_up, w_ffn_conv, w_ffn_down):
    w = _prep_params(dict(
        norm_mix=norm_mix, norm_ffn=norm_ffn, norm_final=norm_final, w_in_even=w_in_even,
        ssm_lam_re=ssm_lam_re, ssm_lam_im=ssm_lam_im, ssm_log_dt=ssm_log_dt, ssm_b_re=ssm_b_re,
        ssm_b_im=ssm_b_im, ssm_c_re=ssm_c_re, ssm_c_im=ssm_c_im, ssm_d=ssm_d, w_glu=w_glu,
        b_glu=b_glu, w_sconv=w_sconv, w_out_even=w_out_even, w_conf_pw1=w_conf_pw1,
        b_conf_pw1=b_conf_pw1, w_conf_dw=w_conf_dw, conf_ln_g=conf_ln_g, conf_ln_b=conf_ln_b,
        w_conf_pw2=w_conf_pw2, b_conf_pw2=b_conf_pw2, w_ffn_up=w_ffn_up, w_ffn_conv=w_ffn_conv,
        w_ffn_down=w_ffn_down))
    depth = norm_mix.shape[0]
    batch, seq, _ = x_prompt.shape
    dec_batch, dec_seq, _ = x_sample.shape
    dt = x_prompt.dtype
    n_even, n_odd = (depth + 1) // 2, depth // 2

    xp = jnp.transpose(x_prompt, (1, 0, 2))
    zp_h = jnp.zeros((n_even, batch, N_SSM_GROUPS, SSM_STATE), dt)
    zp_s = jnp.zeros((n_even, batch, SCONV_W - 1, D_SCONV), dt)
    zp_c = jnp.zeros((n_odd, batch, CONF_W - 1, D_CONF), dt)
    zp_f = jnp.zeros((depth, batch, FFN_W - 1, D_FF), dt)
    yp, *p_states = _trunk(xp, zp_h, zp_h, zp_s, zp_c, zp_f, w,
                           tt=PROMPT_TILE_STEPS, nbt=batch, nbt_odd=batch,
                           meta=meta_tokens.astype(dt))
    y_prompt = jnp.transpose(yp[N_META:], (1, 0, 2))

    xs = jnp.transpose(x_sample, (1, 0, 2))
    ys, *s_states = _trunk(xs, state_ssm_re, state_ssm_im, state_sconv, state_cconv, state_ffn, w,
                           tt=dec_seq, nbt=dec_batch, nbt_odd=SAMPLE_ODD_BATCH_TILE)
    y_sample = jnp.transpose(ys, (1, 0, 2))

    return (y_prompt, y_sample, *p_states, *s_states)
```

```python
import functools
import math

import jax
import jax.numpy as jnp
from jax import lax
from jax.experimental import pallas as pl
from jax.experimental.pallas import tpu as pltpu

D_MODEL = 1024
N_META = 16
SSM_GROUP = 16
SSM_STATE = 64
D_SSM = D_MODEL // 2
N_SSM_GROUPS = D_SSM // SSM_GROUP
D_SCONV = D_MODEL - D_SSM
SCONV_W = 3
D_IN_EVEN = D_SSM + 3 * D_SCONV
D_CONF = D_MODEL
CONF_W = 31
D_FF = ((8 * D_MODEL // 3 + 127) // 128) * 128
FFN_W = 3
EPS = 1e-6

LANES = 128
SUBLANES = 8
KB_COLS = LANES
N_KB = D_SSM // KB_COLS
KB_GROUPS = KB_COLS // SSM_GROUP
KB_STATES = KB_GROUPS * SSM_STATE
D_STATE = N_KB * 2 * KB_STATES
FFN_CHUNK = 256
VMEM_LIMIT_BYTES = 56 * 1024 * 1024
PROMPT_TILE_STEPS = 86
SAMPLE_ODD_BATCH_TILE = 32
CONV_BLOCK_TILES = 8

_F32 = jnp.float32
_BF16 = jnp.bfloat16


def _dot(a, b):
    return jnp.dot(a, b, preferred_element_type=_F32)


def _rms(x, g):
    return (x * lax.rsqrt(jnp.mean(x * x, axis=-1, keepdims=True) + EPS)) * g


def _sigmoid(x):
    return 1.0 / (1.0 + jnp.exp(-x))


def _gelu_tanh(x):
    c = math.sqrt(2.0 / math.pi)
    return x * (0.5 * (1.0 + jnp.tanh(c * (x + 0.044715 * (x * x * x)))))


def _shifted(carry, cur, nb, k, width):
    back = (width - 1 - k) * nb
    if back == 0:
        return cur
    tm = cur.shape[0]
    head = carry[carry.shape[0] - back:]
    if back >= tm:
        return head[:tm]
    return jnp.concatenate([head, cur[:tm - back]], axis=0)


def _short_conv(carry, cur, w, nb):
    width = w.shape[0]
    acc = None
    for k in range(width):
        term = w[k:k + 1, :] * _shifted(carry, cur, nb, k, width)
        acc = term if acc is None else acc + term
    tail = jnp.concatenate([carry, cur], axis=0)[cur.shape[0]:]
    return acc, tail


def _even_kernel(*refs, tt, nb, n_meta):
    if n_meta:
        x_ref, meta_ref, *refs = refs
    else:
        x_ref, *refs = refs
    (h0_ref, s0_ref, g_ref, win_ref, bcat_ref, lam_ref, ccat_ref, dskip_ref, wglu_ref, bglu_ref,
     wsc_ref, wout_ref, xo_ref, ho_ref, so_ref, hcar_ref, scar_ref, hb_ref) = refs
    t_idx = pl.program_id(1)
    n_t = pl.num_programs(1)
    tm = tt * nb

    @pl.when(t_idx == 0)
    def _():
        hcar_ref[...] = h0_ref[...]
        scar_ref[...] = s0_ref[...].reshape(2 * nb, D_SCONV)

    x = x_ref[...].reshape(tm, D_MODEL)
    if n_meta:
        meta = jnp.broadcast_to(meta_ref[...][:, None, :], (n_meta, nb, D_MODEL))
        first = jnp.concatenate([meta.reshape(n_meta * nb, D_MODEL), x[:tm - n_meta * nb]], axis=0)
        x = jnp.where(t_idx == 0, first, x)
    xn = _rms(x, g_ref[...]).astype(_BF16)
    z = _dot(xn, win_ref[...])
    u = z[:, :D_SSM]
    x_b = z[:, D_SSM:D_SSM + D_SCONV]
    g_b = z[:, D_SSM + D_SCONV:D_SSM + 2 * D_SCONV]
    g_c = z[:, D_SSM + 2 * D_SCONV:]
    ub = u.astype(_BF16)

    y_parts = []
    for kb in range(N_KB):
        c0 = kb * 2 * KB_STATES
        hb_ref[...] = _dot(ub[:, kb * KB_COLS:(kb + 1) * KB_COLS], bcat_ref[kb])
        lam_re = jnp.broadcast_to(lam_ref[kb, 0:1, :], (SUBLANES, KB_STATES))
        lam_im = jnp.broadcast_to(lam_ref[kb, 1:2, :], (SUBLANES, KB_STATES))
        for j in range(nb // SUBLANES):
            rows = slice(j * SUBLANES, (j + 1) * SUBLANES)
            h_re0 = hcar_ref[rows, c0:c0 + KB_STATES]
            h_im0 = hcar_ref[rows, c0 + KB_STATES:c0 + 2 * KB_STATES]

            def step(t, carry, j=j, lam_re=lam_re, lam_im=lam_im):
                h_re, h_im = carry
                r = pl.ds(t * nb + j * SUBLANES, SUBLANES)
                n_re = lam_re * h_re - lam_im * h_im + hb_ref[r, 0:KB_STATES]
                n_im = lam_re * h_im + lam_im * h_re + hb_ref[r, KB_STATES:2 * KB_STATES]
                hb_ref[r, 0:KB_STATES] = n_re
                hb_ref[r, KB_STATES:2 * KB_STATES] = n_im
                return n_re, n_im

            h_re, h_im = h_re0, h_im0
            for t in range(tt):
                h_re, h_im = step(t, (h_re, h_im))
            hcar_ref[rows, c0:c0 + KB_STATES] = h_re
            hcar_ref[rows, c0 + KB_STATES:c0 + 2 * KB_STATES] = h_im
        y_parts.append(_dot(hb_ref[...].astype(_BF16), ccat_ref[kb]))
    y = jnp.concatenate(y_parts, axis=1) + dskip_ref[...] * u

    y_a = _gelu_tanh(y)
    y_a = y_a * _sigmoid(_dot(y_a.astype(_BF16), wglu_ref[...]) + bglu_ref[...])

    conv, tail = _short_conv(scar_ref[...], g_c * x_b, wsc_ref[...], nb)
    scar_ref[...] = tail
    y_b = g_b * conv

    mixed = jnp.concatenate([y_a.astype(_BF16), y_b.astype(_BF16)], axis=1)
    xo_ref[...] = (x + _dot(mixed, wout_ref[...])).reshape(tt, nb, D_MODEL)

    @pl.when(t_idx == n_t - 1)
    def _():
        ho_ref[...] = hcar_ref[...]
        so_ref[...] = scar_ref[...].reshape(2, nb, D_SCONV)


def _odd_kernel(x_ref, c0_ref, g_ref, w1_ref, b1_ref, wdw_ref, lng_ref, lnb_ref, w2_ref, b2_ref,
                xo_ref, co_ref,
                win_ref, conv_ref, *, tt, nb, n_t):
    t_idx = pl.program_id(1)
    tm = tt * nb
    halo = (CONF_W - 1) * nb

    @pl.when(t_idx == 0)
    def _():
        win_ref[0:halo, :] = c0_ref[...].reshape(halo, D_CONF)

    x = x_ref[...].reshape(tm, D_MODEL)
    xn = _rms(x, g_ref[...]).astype(_BF16)
    h = _dot(xn, w1_ref[...]) + b1_ref[...]
    win_ref[halo:halo + tm, :] = h[:, :D_CONF] * _sigmoid(h[:, D_CONF:])

    def conv_block(r0, n_tiles):
        view = win_ref.at[pl.ds(r0, halo + n_tiles * SUBLANES), :]
        for c in range(D_CONF // LANES):
            lanes = slice(c * LANES, (c + 1) * LANES)
            accs = [None] * n_tiles
            for k in range(CONF_W):
                tap = wdw_ref[k * SUBLANES:(k + 1) * SUBLANES, lanes]
                for j in range(n_tiles):
                    lo = k * nb + j * SUBLANES
                    term = tap * view[lo:lo + SUBLANES, lanes]
                    accs[j] = term if accs[j] is None else accs[j] + term
            for j in range(n_tiles):
                conv_ref[pl.ds(r0 + j * SUBLANES, SUBLANES), lanes] = accs[j]

    n_row_tiles = tm // SUBLANES
    block_rows = CONV_BLOCK_TILES * SUBLANES

    def conv_trip(i, _):
        conv_block(pl.multiple_of(i * block_rows, block_rows), CONV_BLOCK_TILES)
        return 0

    lax.fori_loop(0, n_row_tiles // CONV_BLOCK_TILES, conv_trip, 0)
    if n_row_tiles % CONV_BLOCK_TILES:
        conv_block(n_row_tiles // CONV_BLOCK_TILES * block_rows, n_row_tiles % CONV_BLOCK_TILES)

    conv = conv_ref[...]
    mu = jnp.mean(conv, axis=-1, keepdims=True)
    xc = conv - mu
    yv = xc * lax.rsqrt(jnp.mean(xc * xc, axis=-1, keepdims=True) + EPS)
    yv = yv * lng_ref[...] + lnb_ref[...]
    act = (yv * _sigmoid(yv)).astype(_BF16)

    out = _dot(act, w2_ref[...]) + b2_ref[...]
    xo_ref[...] = (x + out).reshape(tt, nb, D_MODEL)

    @pl.when(t_idx == n_t - 1)
    def _():
        co_ref[...] = win_ref[tm:tm + halo, :].reshape(CONF_W - 1, nb, D_CONF)

    if n_t > 1:
        assert tt >= CONF_W - 1

        @pl.when(t_idx < n_t - 1)
        def _():
            win_ref[0:halo, :] = win_ref[tm:tm + halo, :]


def _ffn_kernel(x_ref, f0_ref, g_ref, wup_ref, wc_ref, wd_ref, gfin_ref,
                xo_ref, fo_ref,
                fcar_ref, hid_ref, *, tt, nb, final_norm):
    t_idx = pl.program_id(1)
    n_t = pl.num_programs(1)
    tm = tt * nb

    @pl.when(t_idx == 0)
    def _():
        fcar_ref[...] = f0_ref[...].reshape(2 * nb, D_FF)

    x = x_ref[...].reshape(tm, D_MODEL)
    xn = _rms(x, g_ref[...]).astype(_BF16)
    for c in range(D_FF // FFN_CHUNK):
        cols = slice(c * FFN_CHUNK, (c + 1) * FFN_CHUNK)
        a = _dot(xn, wup_ref[:, cols])
        gate = _dot(xn, wup_ref[:, D_FF + c * FFN_CHUNK:D_FF + (c + 1) * FFN_CHUNK])
        a_c, tail = _short_conv(fcar_ref[:, cols], a, wc_ref[:, cols], nb)
        fcar_ref[:, cols] = tail
        hid_ref[:, cols] = ((a_c * _sigmoid(a_c)) * gate).astype(_BF16)
    y = x + _dot(hid_ref[...], wd_ref[...])
    if final_norm:
        y = _rms(y, gfin_ref[...])
    xo_ref[...] = y.reshape(tt, nb, D_MODEL)

    @pl.when(t_idx == n_t - 1)
    def _():
        fo_ref[...] = fcar_ref[...].reshape(2, nb, D_FF)


def _layer_spec(arr, layer):
    tail = (0,) * (arr.ndim - 1)
    return pl.BlockSpec((None,) + arr.shape[1:], lambda b, t: (layer,) + tail,
                        pipeline_mode=pl.Buffered(1))


def _x_spec(tt, nbt):
    return pl.BlockSpec((tt, nbt, D_MODEL), lambda b, t: (t, b, 0))


def _state_spec(steps, nbt, width):
    return pl.BlockSpec((steps, nbt, width), lambda b, t: (0, b, 0))


_PARAMS = pltpu.CompilerParams(dimension_semantics=("arbitrary", "arbitrary"),
                               vmem_limit_bytes=VMEM_LIMIT_BYTES)

_EVEN_KEYS = ("w_in", "bcat", "lam", "ccat", "dskip", "w_glu", "b_glu", "w_sconv", "w_out")
_ODD_KEYS = ("w1", "b1", "w_dw", "ln_g", "ln_b", "w2", "b2")
_FFN_KEYS = ("g_ffn", "w_up", "w_conv", "w_down")


def _even_call(x, h0, s0, w, layer, *, tt, nbt, meta=None):
    n_meta = 0 if meta is None else meta.shape[0]
    length, nb = x.shape[0] + n_meta, x.shape[1]
    grid = (nb // nbt, length // tt)
    w_specs = [_layer_spec(w["g_mix"], layer)] + [_layer_spec(w[k], layer // 2) for k in _EVEN_KEYS]
    if n_meta:
        x_specs = [pl.BlockSpec((pl.Element(tt), pl.Element(nbt), pl.Element(D_MODEL)),
                                lambda b, t: (jnp.maximum(t * tt - n_meta, 0), b * nbt, 0)),
                   pl.BlockSpec(meta.shape, lambda b, t: (0, 0))]
        x_args = (x, meta)
    else:
        x_specs, x_args = [_x_spec(tt, nbt)], (x,)
    return pl.pallas_call(
        functools.partial(_even_kernel, tt=tt, nb=nbt, n_meta=n_meta),
        grid=grid,
        in_specs=x_specs + [pl.BlockSpec((nbt, D_STATE), lambda b, t: (b, 0)),
                            _state_spec(SCONV_W - 1, nbt, D_SCONV)] + w_specs,
        out_specs=[_x_spec(tt, nbt),
                   pl.BlockSpec((nbt, D_STATE), lambda b, t: (b, 0)),
                   _state_spec(SCONV_W - 1, nbt, D_SCONV)],
        out_shape=[jax.ShapeDtypeStruct((length, nb, D_MODEL), _F32),
                   jax.ShapeDtypeStruct(h0.shape, _F32),
                   jax.ShapeDtypeStruct(s0.shape, _F32)],
        scratch_shapes=[pltpu.VMEM((nbt, D_STATE), _F32),
                        pltpu.VMEM(((SCONV_W - 1) * nbt, D_SCONV), _F32),
                        pltpu.VMEM((tt * nbt, 2 * KB_STATES), _F32)],
        compiler_params=_PARAMS,
        name="even_mixer",
    )(*x_args, h0, s0, w["g_mix"], *[w[k] for k in _EVEN_KEYS])


def _odd_call(x, c0, w, layer, *, tt, nbt):
    length, nb, _ = x.shape
    n_t = length // tt
    grid = (nb // nbt, n_t)
    w_specs = [_layer_spec(w["g_mix"], layer)] + [_layer_spec(w[k], layer // 2) for k in _ODD_KEYS]
    return pl.pallas_call(
        functools.partial(_odd_kernel, tt=tt, nb=nbt, n_t=n_t),
        grid=grid,
        in_specs=[_x_spec(tt, nbt), _state_spec(CONF_W - 1, nbt, D_CONF)] + w_specs,
        out_specs=[_x_spec(tt, nbt), _state_spec(CONF_W - 1, nbt, D_CONF)],
        out_shape=[jax.ShapeDtypeStruct(x.shape, _F32), jax.ShapeDtypeStruct(c0.shape, _F32)],
        scratch_shapes=[pltpu.VMEM(((CONF_W - 1 + tt) * nbt, D_CONF), _F32),
                        pltpu.VMEM((tt * nbt, D_CONF), _F32)],
        compiler_params=_PARAMS,
        name="odd_mixer",
    )(x, c0, w["g_mix"], *[w[k] for k in _ODD_KEYS])


def _ffn_call(x, f0, w, layer, *, tt, nbt, final_norm):
    length, nb, _ = x.shape
    grid = (nb // nbt, length // tt)
    w_specs = [_layer_spec(w[k], layer) for k in _FFN_KEYS] + [_layer_spec(w["g_final"], 0)]
    return pl.pallas_call(
        functools.partial(_ffn_kernel, tt=tt, nb=nbt, final_norm=final_norm),
        grid=grid,
        in_specs=[_x_spec(tt, nbt), _state_spec(FFN_W - 1, nbt, D_FF)] + w_specs,
        out_specs=[_x_spec(tt, nbt), _state_spec(FFN_W - 1, nbt, D_FF)],
        out_shape=[jax.ShapeDtypeStruct(x.shape, _F32), jax.ShapeDtypeStruct(f0.shape, _F32)],
        scratch_shapes=[pltpu.VMEM(((FFN_W - 1) * nbt, D_FF), _F32),
                        pltpu.VMEM((tt * nbt, D_FF), _BF16)],
        compiler_params=_PARAMS,
        name="conv_ffn",
    )(x, f0, *[w[k] for k in _FFN_KEYS], w["g_final"])


def _rows(v):
    return v.reshape(v.shape[0], 1, v.shape[-1]).astype(_F32)


def _prep_params(p):
    n_even = p["w_in_even"].shape[0]
    p_, s_ = SSM_STATE, SSM_GROUP
    lam = lax.complex(p["ssm_lam_re"].astype(_F32), p["ssm_lam_im"].astype(_F32))
    dt = jnp.exp(p["ssm_log_dt"].astype(_F32))[..., None]
    lam_bar = jnp.exp(lam * dt)
    b_bar = ((lam_bar - 1.0) / lam)[..., None] * lax.complex(
        p["ssm_b_re"].astype(_F32), p["ssm_b_im"].astype(_F32))
    eye = jnp.eye(KB_GROUPS, dtype=_F32)

    def b_block(v):
        v = v.reshape(n_even, N_KB, KB_GROUPS, p_, s_)
        return jnp.einsum("ekgpi,gh->ekgihp", v, eye).reshape(n_even, N_KB, KB_COLS, KB_STATES)

    def c_block(v):
        v = v.reshape(n_even, N_KB, KB_GROUPS, s_, p_)
        return jnp.einsum("ekgip,gh->ekgphi", v, eye).reshape(n_even, N_KB, KB_STATES, KB_COLS)

    bcat = jnp.concatenate([b_block(jnp.real(b_bar)), b_block(jnp.imag(b_bar))], axis=3)
    ccat = jnp.concatenate([c_block(p["ssm_c_re"].astype(_F32)),
                            -c_block(p["ssm_c_im"].astype(_F32))], axis=2)
    lamcat = jnp.stack([jnp.real(lam_bar).reshape(n_even, N_KB, KB_STATES),
                        jnp.imag(lam_bar).reshape(n_even, N_KB, KB_STATES)], axis=2)
    return dict(
        g_mix=_rows(p["norm_mix"]), g_ffn=_rows(p["norm_ffn"]), g_final=_rows(p["norm_final"][None]),
        w_in=p["w_in_even"].astype(_BF16), bcat=bcat.astype(_BF16), lam=lamcat,
        ccat=ccat.astype(_BF16), dskip=_rows(p["ssm_d"]), w_glu=p["w_glu"].astype(_BF16),
        b_glu=_rows(p["b_glu"]), w_sconv=p["w_sconv"].astype(_F32),
        w_out=p["w_out_even"].astype(_BF16),
        w1=p["w_conf_pw1"].astype(_BF16), b1=_rows(p["b_conf_pw1"]),
        w_dw=jnp.repeat(p["w_conf_dw"].astype(_F32), SUBLANES, axis=1),
        ln_g=_rows(p["conf_ln_g"]), ln_b=_rows(p["conf_ln_b"]),
        w2=p["w_conf_pw2"].astype(_BF16), b2=_rows(p["b_conf_pw2"]),
        w_up=p["w_ffn_up"].astype(_BF16), w_conv=p["w_ffn_conv"].astype(_F32),
        w_down=p["w_ffn_down"].astype(_BF16))


def _time_major(v):
    return jnp.transpose(v, (1, 0, 2))


def _ssm_to_flat(re, im):
    nb = re.shape[0]
    re = re.reshape(nb, N_KB, KB_STATES)
    im = im.reshape(nb, N_KB, KB_STATES)
    return jnp.concatenate([re, im], axis=2).reshape(nb, D_STATE).astype(_F32)


def _flat_to_ssm(h):
    nb = h.shape[0]
    h = h.reshape(nb, N_KB, 2, KB_STATES)
    shape = (nb, N_SSM_GROUPS, SSM_STATE)
    return h[:, :, 0].reshape(shape), h[:, :, 1].reshape(shape)


def _trunk(x, ssm_re, ssm_im, sconv, cconv, ffn_buf, w, *, tt, nbt, nbt_odd, meta=None):
    depth = w["g_mix"].shape[0]
    n_re, n_im, n_s, n_c, n_f = [], [], [], [], []
    for l in range(depth):
        i = l // 2
        if l % 2 == 0:
            h0 = _ssm_to_flat(ssm_re[i], ssm_im[i])
            x, h1, s1 = _even_call(x, h0, _time_major(sconv[i]), w, l, tt=tt, nbt=nbt,
                                   meta=meta if l == 0 else None)
            re, im = _flat_to_ssm(h1)
            n_re.append(re)
            n_im.append(im)
            n_s.append(_time_major(s1))
        else:
            x, c1 = _odd_call(x, _time_major(cconv[i]), w, l, tt=tt, nbt=nbt_odd)
            n_c.append(_time_major(c1))
        x, f1 = _ffn_call(x, _time_major(ffn_buf[l]), w, l, tt=tt, nbt=nbt, final_norm=(l == depth - 1))
        n_f.append(_time_major(f1))
    return x, jnp.stack(n_re), jnp.stack(n_im), jnp.stack(n_s), jnp.stack(n_c), jnp.stack(n_f)


def kernel(x_prompt, x_sample, state_ssm_re, state_ssm_im, state_sconv, state_cconv, state_ffn, meta_tokens, norm_mix, norm_ffn, norm_final, w_in_even, ssm_lam_re, ssm_lam_im, ssm_log_dt, ssm_b_re, ssm_b_im, ssm_c_re, ssm_c_im, ssm_d, w_glu, b_glu, w_sconv, w_out_even, w_conf_pw1, b_conf_pw1, w_conf_dw, conf_ln_g, conf_ln_b, w_conf_pw2, b_conf_pw2, w_ffn_up, w_ffn_conv, w_ffn_down):
    w = _prep_params(dict(
        norm_mix=norm_mix, norm_ffn=norm_ffn, norm_final=norm_final, w_in_even=w_in_even,
        ssm_lam_re=ssm_lam_re, ssm_lam_im=ssm_lam_im, ssm_log_dt=ssm_log_dt, ssm_b_re=ssm_b_re,
        ssm_b_im=ssm_b_im, ssm_c_re=ssm_c_re, ssm_c_im=ssm_c_im, ssm_d=ssm_d, w_glu=w_glu,
        b_glu=b_glu, w_sconv=w_sconv, w_out_even=w_out_even, w_conf_pw1=w_conf_pw1,
        b_conf_pw1=b_conf_pw1, w_conf_dw=w_conf_dw, conf_ln_g=conf_ln_g, conf_ln_b=conf_ln_b,
        w_conf_pw2=w_conf_pw2, b_conf_pw2=b_conf_pw2, w_ffn_up=w_ffn_up, w_ffn_conv=w_ffn_conv,
        w_ffn_down=w_ffn_down))
    depth = norm_mix.shape[0]
    batch, seq, _ = x_prompt.shape
    dec_batch, dec_seq, _ = x_sample.shape
    dt = x_prompt.dtype
    n_even, n_odd = (depth + 1) // 2, depth // 2

    xp = jnp.transpose(x_prompt, (1, 0, 2))
    zp_h = jnp.zeros((n_even, batch, N_SSM_GROUPS, SSM_STATE), dt)
    zp_s = jnp.zeros((n_even, batch, SCONV_W - 1, D_SCONV), dt)
    zp_c = jnp.zeros((n_odd, batch, CONF_W - 1, D_CONF), dt)
    zp_f = jnp.zeros((depth, batch, FFN_W - 1, D_FF), dt)
    yp, *p_states = _trunk(xp, zp_h, zp_h, zp_s, zp_c, zp_f, w,
                           tt=PROMPT_TILE_STEPS, nbt=batch, nbt_odd=batch,
                           meta=meta_tokens.astype(dt))
    y_prompt = jnp.transpose(yp[N_META:], (1, 0, 2))

    xs = jnp.transpose(x_sample, (1, 0, 2))
    ys, *s_states = _trunk(xs, state_ssm_re, state_ssm_im, state_sconv, state_cconv, state_ffn, w,
                           tt=dec_seq, nbt=dec_batch, nbt_odd=SAMPLE_ODD_BATCH_TILE)
    y_sample = jnp.transpose(ys, (1, 0, 2))

    return (y_prompt, y_sample, *p_states, *s_states)
```

```python
import functools
import math

import jax
import jax.numpy as jnp
from jax import lax
from jax.experimental import pallas as pl
from jax.experimental.pallas import tpu as pltpu

D_MODEL = 1024
N_META = 16
SSM_GROUP = 16
SSM_STATE = 64
D_SSM = D_MODEL // 2
N_SSM_GROUPS = D_SSM // SSM_GROUP
D_SCONV = D_MODEL - D_SSM
SCONV_W = 3
D_IN_EVEN = D_SSM + 3 * D_SCONV
D_CONF = D_MODEL
CONF_W = 31
D_FF = ((8 * D_MODEL // 3 + 127) // 128) * 128
FFN_W = 3
EPS = 1e-6

LANES = 128
SUBLANES = 8
KB_COLS = LANES
N_KB = D_SSM // KB_COLS
KB_GROUPS = KB_COLS // SSM_GROUP
KB_STATES = KB_GROUPS * SSM_STATE
D_STATE = N_KB * 2 * KB_STATES
FFN_CHUNK = 256
VMEM_LIMIT_BYTES = 56 * 1024 * 1024
PROMPT_TILE_STEPS = 86
PROMPT_MIXER_TILE_STEPS = 129
SAMPLE_ODD_BATCH_TILE = 32
CONV_BLOCK_TILES = 8

_F32 = jnp.float32
_BF16 = jnp.bfloat16


def _dot(a, b):
    return jnp.dot(a, b, preferred_element_type=_F32)


def _rms(x, g):
    return (x * lax.rsqrt(jnp.mean(x * x, axis=-1, keepdims=True) + EPS)) * g


def _sigmoid(x):
    return 1.0 / (1.0 + jnp.exp(-x))


def _gelu_tanh(x):
    c = math.sqrt(2.0 / math.pi)
    return x * (0.5 * (1.0 + jnp.tanh(c * (x + 0.044715 * (x * x * x)))))


def _shifted(carry, cur, nb, k, width):
    back = (width - 1 - k) * nb
    if back == 0:
        return cur
    tm = cur.shape[0]
    head = carry[carry.shape[0] - back:]
    if back >= tm:
        return head[:tm]
    return jnp.concatenate([head, cur[:tm - back]], axis=0)


def _short_conv(carry, cur, w, nb):
    width = w.shape[0]
    acc = None
    for k in range(width):
        term = w[k:k + 1, :] * _shifted(carry, cur, nb, k, width)
        acc = term if acc is None else acc + term
    tail = jnp.concatenate([carry, cur], axis=0)[cur.shape[0]:]
    return acc, tail


def _even_kernel(*refs, tt, nb, n_meta):
    if n_meta:
        x_ref, meta_ref, *refs = refs
    else:
        x_ref, *refs = refs
    (h0_ref, s0_ref, g_ref, win_ref, bcat_ref, lam_ref, ccat_ref, dskip_ref, wglu_ref, bglu_ref,
     wsc_ref, wout_ref, xo_ref, ho_ref, so_ref, hcar_ref, scar_ref, hb_ref) = refs
    t_idx = pl.program_id(1)
    n_t = pl.num_programs(1)
    tm = tt * nb

    @pl.when(t_idx == 0)
    def _():
        hcar_ref[...] = h0_ref[...]
        scar_ref[...] = s0_ref[...].reshape(2 * nb, D_SCONV)

    x = x_ref[...].reshape(tm, D_MODEL)
    if n_meta:
        meta = jnp.broadcast_to(meta_ref[...][:, None, :], (n_meta, nb, D_MODEL))
        first = jnp.concatenate([meta.reshape(n_meta * nb, D_MODEL), x[:tm - n_meta * nb]], axis=0)
        x = jnp.where(t_idx == 0, first, x)
    xn = _rms(x, g_ref[...]).astype(_BF16)
    u = _dot(xn, win_ref[:, :D_SSM])
    ub = u.astype(_BF16)

    y_parts = []
    for kb in range(N_KB):
        c0 = kb * 2 * KB_STATES
        hb_ref[...] = _dot(ub[:, kb * KB_COLS:(kb + 1) * KB_COLS], bcat_ref[kb])
        lam_re = jnp.broadcast_to(lam_ref[kb, 0:1, :], (SUBLANES, KB_STATES))
        lam_im = jnp.broadcast_to(lam_ref[kb, 1:2, :], (SUBLANES, KB_STATES))
        for j in range(nb // SUBLANES):
            rows = slice(j * SUBLANES, (j + 1) * SUBLANES)
            h_re0 = hcar_ref[rows, c0:c0 + KB_STATES]
            h_im0 = hcar_ref[rows, c0 + KB_STATES:c0 + 2 * KB_STATES]

            def step(t, carry, j=j, lam_re=lam_re, lam_im=lam_im):
                h_re, h_im = carry
                r = pl.ds(t * nb + j * SUBLANES, SUBLANES)
                n_re = lam_re * h_re - lam_im * h_im + hb_ref[r, 0:KB_STATES]
                n_im = lam_re * h_im + lam_im * h_re + hb_ref[r, KB_STATES:2 * KB_STATES]
                hb_ref[r, 0:KB_STATES] = n_re
                hb_ref[r, KB_STATES:2 * KB_STATES] = n_im
                return n_re, n_im

            h_re, h_im = h_re0, h_im0
            for t in range(tt):
                h_re, h_im = step(t, (h_re, h_im))
            hcar_ref[rows, c0:c0 + KB_STATES] = h_re
            hcar_ref[rows, c0 + KB_STATES:c0 + 2 * KB_STATES] = h_im
        y_parts.append(_dot(hb_ref[...].astype(_BF16), ccat_ref[kb]))
    y = jnp.concatenate(y_parts, axis=1) + dskip_ref[...] * u

    y_a = _gelu_tanh(y)
    y_a = y_a * _sigmoid(_dot(y_a.astype(_BF16), wglu_ref[...]) + bglu_ref[...])

    x_b = _dot(xn, win_ref[:, D_SSM:D_SSM + D_SCONV])
    g_c = _dot(xn, win_ref[:, D_SSM + 2 * D_SCONV:])
    g_b = _dot(xn, win_ref[:, D_SSM + D_SCONV:D_SSM + 2 * D_SCONV])
    conv, tail = _short_conv(scar_ref[...], g_c * x_b, wsc_ref[...], nb)
    scar_ref[...] = tail
    y_b = g_b * conv

    mixed = jnp.concatenate([y_a.astype(_BF16), y_b.astype(_BF16)], axis=1)
    xo_ref[...] = (x + _dot(mixed, wout_ref[...])).reshape(tt, nb, D_MODEL)

    @pl.when(t_idx == n_t - 1)
    def _():
        ho_ref[...] = hcar_ref[...]
        so_ref[...] = scar_ref[...].reshape(2, nb, D_SCONV)


def _odd_kernel(x_ref, c0_ref, g_ref, w1_ref, b1_ref, wdw_ref, lng_ref, lnb_ref, w2_ref, b2_ref,
                xo_ref, co_ref,
                win_ref, conv_ref, *, tt, nb, n_t):
    t_idx = pl.program_id(1)
    tm = tt * nb
    halo = (CONF_W - 1) * nb

    @pl.when(t_idx == 0)
    def _():
        win_ref[0:halo, :] = c0_ref[...].reshape(halo, D_CONF)

    x = x_ref[...].reshape(tm, D_MODEL)
    xn = _rms(x, g_ref[...]).astype(_BF16)
    for c in range(D_CONF // LANES):
        h = _dot(xn, w1_ref[c]) + b1_ref[c]
        win_ref[halo:halo + tm, c * LANES:(c + 1) * LANES] = h[:, :LANES] * _sigmoid(h[:, LANES:])

    def conv_block(r0, n_tiles):
        view = win_ref.at[pl.ds(r0, halo + n_tiles * SUBLANES), :]
        for c in range(D_CONF // LANES):
            lanes = slice(c * LANES, (c + 1) * LANES)
            accs = [None] * n_tiles
            for k in range(CONF_W):
                tap = wdw_ref[k * SUBLANES:(k + 1) * SUBLANES, lanes]
                for j in range(n_tiles):
                    lo = k * nb + j * SUBLANES
                    term = tap * view[lo:lo + SUBLANES, lanes]
                    accs[j] = term if accs[j] is None else accs[j] + term
            for j in range(n_tiles):
                conv_ref[pl.ds(r0 + j * SUBLANES, SUBLANES), lanes] = accs[j]

    n_row_tiles = tm // SUBLANES
    block_rows = CONV_BLOCK_TILES * SUBLANES

    def conv_trip(i, _):
        conv_block(pl.multiple_of(i * block_rows, block_rows), CONV_BLOCK_TILES)
        return 0

    lax.fori_loop(0, n_row_tiles // CONV_BLOCK_TILES, conv_trip, 0)
    if n_row_tiles % CONV_BLOCK_TILES:
        conv_block(n_row_tiles // CONV_BLOCK_TILES * block_rows, n_row_tiles % CONV_BLOCK_TILES)

    conv = conv_ref[...]
    mu = jnp.mean(conv, axis=-1, keepdims=True)
    xc = conv - mu
    yv = xc * lax.rsqrt(jnp.mean(xc * xc, axis=-1, keepdims=True) + EPS)
    yv = yv * lng_ref[...] + lnb_ref[...]
    act = (yv * _sigmoid(yv)).astype(_BF16)

    out = _dot(act, w2_ref[...]) + b2_ref[...]
    xo_ref[...] = (x + out).reshape(tt, nb, D_MODEL)

    @pl.when(t_idx == n_t - 1)
    def _():
        co_ref[...] = win_ref[tm:tm + halo, :].reshape(CONF_W - 1, nb, D_CONF)

    if n_t > 1:
        assert tt >= CONF_W - 1

        @pl.when(t_idx < n_t - 1)
        def _():
            win_ref[0:halo, :] = win_ref[tm:tm + halo, :]


def _ffn_kernel(x_ref, f0_ref, g_ref, wup_ref, wc_ref, wd_ref, gfin_ref,
                xo_ref, fo_ref,
                fcar_ref, hid_ref, *, tt, nb, final_norm):
    t_idx = pl.program_id(1)
    n_t = pl.num_programs(1)
    tm = tt * nb

    @pl.when(t_idx == 0)
    def _():
        fcar_ref[...] = f0_ref[...].reshape(2 * nb, D_FF)

    x = x_ref[...].reshape(tm, D_MODEL)
    xn = _rms(x, g_ref[...]).astype(_BF16)
    for c in range(D_FF // FFN_CHUNK):
        cols = slice(c * FFN_CHUNK, (c + 1) * FFN_CHUNK)
        a = _dot(xn, wup_ref[:, cols])
        gate = _dot(xn, wup_ref[:, D_FF + c * FFN_CHUNK:D_FF + (c + 1) * FFN_CHUNK])
        a_c, tail = _short_conv(fcar_ref[:, cols], a, wc_ref[:, cols], nb)
        fcar_ref[:, cols] = tail
        hid_ref[:, cols] = ((a_c * _sigmoid(a_c)) * gate).astype(_BF16)
    y = x + _dot(hid_ref[...], wd_ref[...])
    if final_norm:
        y = _rms(y, gfin_ref[...])
    xo_ref[...] = y.reshape(tt, nb, D_MODEL)

    @pl.when(t_idx == n_t - 1)
    def _():
        fo_ref[...] = fcar_ref[...].reshape(2, nb, D_FF)


def _layer_spec(arr, layer):
    tail = (0,) * (arr.ndim - 1)
    return pl.BlockSpec((None,) + arr.shape[1:], lambda b, t: (layer,) + tail,
                        pipeline_mode=pl.Buffered(1))


def _x_spec(tt, nbt):
    return pl.BlockSpec((tt, nbt, D_MODEL), lambda b, t: (t, b, 0))


def _state_spec(steps, nbt, width):
    return pl.BlockSpec((steps, nbt, width), lambda b, t: (0, b, 0))


_PARAMS = pltpu.CompilerParams(dimension_semantics=("arbitrary", "arbitrary"),
                               vmem_limit_bytes=VMEM_LIMIT_BYTES)

_EVEN_KEYS = ("w_in", "bcat", "lam", "ccat", "dskip", "w_glu", "b_glu", "w_sconv", "w_out")
_ODD_KEYS = ("w1", "b1", "w_dw", "ln_g", "ln_b", "w2", "b2")
_FFN_KEYS = ("g_ffn", "w_up", "w_conv", "w_down")


def _even_call(x, h0, s0, w, layer, *, tt, nbt, meta=None):
    n_meta = 0 if meta is None else meta.shape[0]
    length, nb = x.shape[0] + n_meta, x.shape[1]
    grid = (nb // nbt, length // tt)
    w_specs = [_layer_spec(w["g_mix"], layer)] + [_layer_spec(w[k], layer // 2) for k in _EVEN_KEYS]
    if n_meta:
        x_specs = [pl.BlockSpec((pl.Element(tt), pl.Element(nbt), pl.Element(D_MODEL)),
                                lambda b, t: (jnp.maximum(t * tt - n_meta, 0), b * nbt, 0)),
                   pl.BlockSpec(meta.shape, lambda b, t: (0, 0))]
        x_args = (x, meta)
    else:
        x_specs, x_args = [_x_spec(tt, nbt)], (x,)
    return pl.pallas_call(
        functools.partial(_even_kernel, tt=tt, nb=nbt, n_meta=n_meta),
        grid=grid,
        in_specs=x_specs + [pl.BlockSpec((nbt, D_STATE), lambda b, t: (b, 0)),
                            _state_spec(SCONV_W - 1, nbt, D_SCONV)] + w_specs,
        out_specs=[_x_spec(tt, nbt),
                   pl.BlockSpec((nbt, D_STATE), lambda b, t: (b, 0)),
                   _state_spec(SCONV_W - 1, nbt, D_SCONV)],
        out_shape=[jax.ShapeDtypeStruct((length, nb, D_MODEL), _F32),
                   jax.ShapeDtypeStruct(h0.shape, _F32),
                   jax.ShapeDtypeStruct(s0.shape, _F32)],
        scratch_shapes=[pltpu.VMEM((nbt, D_STATE), _F32),
                        pltpu.VMEM(((SCONV_W - 1) * nbt, D_SCONV), _F32),
                        pltpu.VMEM((tt * nbt, 2 * KB_STATES), _F32)],
        compiler_params=_PARAMS,
        name="even_mixer",
    )(*x_args, h0, s0, w["g_mix"], *[w[k] for k in _EVEN_KEYS])


def _odd_call(x, c0, w, layer, *, tt, nbt):
    length, nb, _ = x.shape
    n_t = length // tt
    grid = (nb // nbt, n_t)
    w_specs = [_layer_spec(w["g_mix"], layer)] + [_layer_spec(w[k], layer // 2) for k in _ODD_KEYS]
    return pl.pallas_call(
        functools.partial(_odd_kernel, tt=tt, nb=nbt, n_t=n_t),
        grid=grid,
        in_specs=[_x_spec(tt, nbt), _state_spec(CONF_W - 1, nbt, D_CONF)] + w_specs,
        out_specs=[_x_spec(tt, nbt), _state_spec(CONF_W - 1, nbt, D_CONF)],
        out_shape=[jax.ShapeDtypeStruct(x.shape, _F32), jax.ShapeDtypeStruct(c0.shape, _F32)],
        scratch_shapes=[pltpu.VMEM(((CONF_W - 1 + tt) * nbt, D_CONF), _F32),
                        pltpu.VMEM((tt * nbt, D_CONF), _F32)],
        compiler_params=_PARAMS,
        name="odd_mixer",
    )(x, c0, w["g_mix"], *[w[k] for k in _ODD_KEYS])


def _ffn_call(x, f0, w, layer, *, tt, nbt, final_norm):
    length, nb, _ = x.shape
    grid = (nb // nbt, length // tt)
    w_specs = [_layer_spec(w[k], layer) for k in _FFN_KEYS] + [_layer_spec(w["g_final"], 0)]
    return pl.pallas_call(
        functools.partial(_ffn_kernel, tt=tt, nb=nbt, final_norm=final_norm),
        grid=grid,
        in_specs=[_x_spec(tt, nbt), _state_spec(FFN_W - 1, nbt, D_FF)] + w_specs,
        out_specs=[_x_spec(tt, nbt), _state_spec(FFN_W - 1, nbt, D_FF)],
        out_shape=[jax.ShapeDtypeStruct(x.shape, _F32), jax.ShapeDtypeStruct(f0.shape, _F32)],
        scratch_shapes=[pltpu.VMEM(((FFN_W - 1) * nbt, D_FF), _F32),
                        pltpu.VMEM((tt * nbt, D_FF), _BF16)],
        compiler_params=_PARAMS,
        name="conv_ffn",
    )(x, f0, *[w[k] for k in _FFN_KEYS], w["g_final"])


def _rows(v):
    return v.reshape(v.shape[0], 1, v.shape[-1]).astype(_F32)


def _glu_columns(v):
    n, rows, _ = v.shape
    v = v.reshape(n, rows, 2, D_CONF // LANES, LANES)
    return jnp.transpose(v, (0, 3, 1, 2, 4)).reshape(n, D_CONF // LANES, rows, 2 * LANES)


def _prep_params(p):
    n_even = p["w_in_even"].shape[0]
    p_, s_ = SSM_STATE, SSM_GROUP
    lam = lax.complex(p["ssm_lam_re"].astype(_F32), p["ssm_lam_im"].astype(_F32))
    dt = jnp.exp(p["ssm_log_dt"].astype(_F32))[..., None]
    lam_bar = jnp.exp(lam * dt)
    b_bar = ((lam_bar - 1.0) / lam)[..., None] * lax.complex(
        p["ssm_b_re"].astype(_F32), p["ssm_b_im"].astype(_F32))
    eye = jnp.eye(KB_GROUPS, dtype=_F32)

    def b_block(v):
        v = v.reshape(n_even, N_KB, KB_GROUPS, p_, s_)
        return jnp.einsum("ekgpi,gh->ekgihp", v, eye).reshape(n_even, N_KB, KB_COLS, KB_STATES)

    def c_block(v):
        v = v.reshape(n_even, N_KB, KB_GROUPS, s_, p_)
        return jnp.einsum("ekgip,gh->ekgphi", v, eye).reshape(n_even, N_KB, KB_STATES, KB_COLS)

    bcat = jnp.concatenate([b_block(jnp.real(b_bar)), b_block(jnp.imag(b_bar))], axis=3)
    ccat = jnp.concatenate([c_block(p["ssm_c_re"].astype(_F32)),
                            -c_block(p["ssm_c_im"].astype(_F32))], axis=2)
    lamcat = jnp.stack([jnp.real(lam_bar).reshape(n_even, N_KB, KB_STATES),
                        jnp.imag(lam_bar).reshape(n_even, N_KB, KB_STATES)], axis=2)
    return dict(
        g_mix=_rows(p["norm_mix"]), g_ffn=_rows(p["norm_ffn"]), g_final=_rows(p["norm_final"][None]),
        w_in=p["w_in_even"].astype(_BF16), bcat=bcat.astype(_BF16), lam=lamcat,
        ccat=ccat.astype(_BF16), dskip=_rows(p["ssm_d"]), w_glu=p["w_glu"].astype(_BF16),
        b_glu=_rows(p["b_glu"]), w_sconv=p["w_sconv"].astype(_F32),
        w_out=p["w_out_even"].astype(_BF16),
        w1=_glu_columns(p["w_conf_pw1"].astype(_BF16)), b1=_glu_columns(_rows(p["b_conf_pw1"])),
        w_dw=jnp.repeat(p["w_conf_dw"].astype(_F32), SUBLANES, axis=1),
        ln_g=_rows(p["conf_ln_g"]), ln_b=_rows(p["conf_ln_b"]),
        w2=p["w_conf_pw2"].astype(_BF16), b2=_rows(p["b_conf_pw2"]),
        w_up=p["w_ffn_up"].astype(_BF16), w_conv=p["w_ffn_conv"].astype(_F32),
        w_down=p["w_ffn_down"].astype(_BF16))


def _time_major(v):
    return jnp.transpose(v, (1, 0, 2))


def _ssm_to_flat(re, im):
    nb = re.shape[0]
    re = re.reshape(nb, N_KB, KB_STATES)
    im = im.reshape(nb, N_KB, KB_STATES)
    return jnp.concatenate([re, im], axis=2).reshape(nb, D_STATE).astype(_F32)


def _flat_to_ssm(h):
    nb = h.shape[0]
    h = h.reshape(nb, N_KB, 2, KB_STATES)
    shape = (nb, N_SSM_GROUPS, SSM_STATE)
    return h[:, :, 0].reshape(shape), h[:, :, 1].reshape(shape)


def _trunk(x, ssm_re, ssm_im, sconv, cconv, ffn_buf, w, *, tt, tt_mix, nbt, nbt_odd, meta=None):
    depth = w["g_mix"].shape[0]
    n_re, n_im, n_s, n_c, n_f = [], [], [], [], []
    for l in range(depth):
        i = l // 2
        if l % 2 == 0:
            h0 = _ssm_to_flat(ssm_re[i], ssm_im[i])
            x, h1, s1 = _even_call(x, h0, _time_major(sconv[i]), w, l, tt=tt_mix, nbt=nbt,
                                   meta=meta if l == 0 else None)
            re, im = _flat_to_ssm(h1)
            n_re.append(re)
            n_im.append(im)
            n_s.append(_time_major(s1))
        else:
            x, c1 = _odd_call(x, _time_major(cconv[i]), w, l, tt=tt_mix, nbt=nbt_odd)
            n_c.append(_time_major(c1))
        x, f1 = _ffn_call(x, _time_major(ffn_buf[l]), w, l, tt=tt, nbt=nbt, final_norm=(l == depth - 1))
        n_f.append(_time_major(f1))
    return x, jnp.stack(n_re), jnp.stack(n_im), jnp.stack(n_s), jnp.stack(n_c), jnp.stack(n_f)


def kernel(x_prompt, x_sample, state_ssm_re, state_ssm_im, state_sconv, state_cconv, state_ffn, meta_tokens, norm_mix, norm_ffn, norm_final, w_in_even, ssm_lam_re, ssm_lam_im, ssm_log_dt, ssm_b_re, ssm_b_im, ssm_c_re, ssm_c_im, ssm_d, w_glu, b_glu, w_sconv, w_out_even, w_conf_pw1, b_conf_pw1, w_conf_dw, conf_ln_g, conf_ln_b, w_conf_pw2, b_conf_pw2, w_ffn_up, w_ffn_conv, w_ffn_down):
    w = _prep_params(dict(
        norm_mix=norm_mix, norm_ffn=norm_ffn, norm_final=norm_final, w_in_even=w_in_even,
        ssm_lam_re=ssm_lam_re, ssm_lam_im=ssm_lam_im, ssm_log_dt=ssm_log_dt, ssm_b_re=ssm_b_re,
        ssm_b_im=ssm_b_im, ssm_c_re=ssm_c_re, ssm_c_im=ssm_c_im, ssm_d=ssm_d, w_glu=w_glu,
        b_glu=b_glu, w_sconv=w_sconv, w_out_even=w_out_even, w_conf_pw1=w_conf_pw1,
        b_conf_pw1=b_conf_pw1, w_conf_dw=w_conf_dw, conf_ln_g=conf_ln_g, conf_ln_b=conf_ln_b,
        w_conf_pw2=w_conf_pw2, b_conf_pw2=b_conf_pw2, w_ffn_up=w_ffn_up, w_ffn_conv=w_ffn_conv,
        w_ffn_down=w_ffn_down))
    depth = norm_mix.shape[0]
    batch, seq, _ = x_prompt.shape
    dec_batch, dec_seq, _ = x_sample.shape
    dt = x_prompt.dtype
    n_even, n_odd = (depth + 1) // 2, depth // 2

    xp = jnp.transpose(x_prompt, (1, 0, 2))
    zp_h = jnp.zeros((n_even, batch, N_SSM_GROUPS, SSM_STATE), dt)
    zp_s = jnp.zeros((n_even, batch, SCONV_W - 1, D_SCONV), dt)
    zp_c = jnp.zeros((n_odd, batch, CONF_W - 1, D_CONF), dt)
    zp_f = jnp.zeros((depth, batch, FFN_W - 1, D_FF), dt)
    yp, *p_states = _trunk(xp, zp_h, zp_h, zp_s, zp_c, zp_f, w,
                           tt=PROMPT_TILE_STEPS, tt_mix=PROMPT_MIXER_TILE_STEPS, nbt=batch, nbt_odd=batch,
                           meta=meta_tokens.astype(dt))
    y_prompt = jnp.transpose(yp[N_META:], (1, 0, 2))

    xs = jnp.transpose(x_sample, (1, 0, 2))
    ys, *s_states = _trunk(xs, state_ssm_re, state_ssm_im, state_sconv, state_cconv, state_ffn, w,
                           tt=dec_seq, tt_mix=dec_seq, nbt=dec_batch, nbt_odd=SAMPLE_ODD_BATCH_TILE)
    y_sample = jnp.transpose(ys, (1, 0, 2))

    return (y_prompt, y_sample, *p_states, *s_states)
```

```python
import functools
import math

import jax
import jax.numpy as jnp
from jax import lax
from jax.experimental import pallas as pl
from jax.experimental.pallas import tpu as pltpu

D_MODEL = 1024
N_META = 16
SSM_GROUP = 16
SSM_STATE = 64
D_SSM = D_MODEL // 2
N_SSM_GROUPS = D_SSM // SSM_GROUP
D_SCONV = D_MODEL - D_SSM
SCONV_W = 3
D_IN_EVEN = D_SSM + 3 * D_SCONV
D_CONF = D_MODEL
CONF_W = 31
D_FF = ((8 * D_MODEL // 3 + 127) // 128) * 128
FFN_W = 3
EPS = 1e-6

LANES = 128
SUBLANES = 8
KB_COLS = LANES
N_KB = D_SSM // KB_COLS
KB_GROUPS = KB_COLS // SSM_GROUP
KB_STATES = KB_GROUPS * SSM_STATE
D_STATE = N_KB * 2 * KB_STATES
FFN_CHUNK = 256
PW1_SLICE = 256
VMEM_LIMIT_BYTES = 56 * 1024 * 1024
PROMPT_TILE_STEPS = 86
PROMPT_MIXER_TILE_STEPS = 129
SAMPLE_ODD_BATCH_TILE = 32
CONV_BLOCK_TILES = 8

_F32 = jnp.float32
_BF16 = jnp.bfloat16


def _dot(a, b):
    return jnp.dot(a, b, preferred_element_type=_F32)


def _rms(x, g):
    return (x * lax.rsqrt(jnp.mean(x * x, axis=-1, keepdims=True) + EPS)) * g


def _sigmoid(x):
    return 1.0 / (1.0 + jnp.exp(-x))


def _gelu_tanh(x):
    c = math.sqrt(2.0 / math.pi)
    return x * (0.5 * (1.0 + jnp.tanh(c * (x + 0.044715 * (x * x * x)))))


def _shifted(carry, cur, nb, k, width):
    back = (width - 1 - k) * nb
    if back == 0:
        return cur
    tm = cur.shape[0]
    head = carry[carry.shape[0] - back:]
    if back >= tm:
        return head[:tm]
    return jnp.concatenate([head, cur[:tm - back]], axis=0)


def _short_conv(carry, cur, w, nb):
    width = w.shape[0]
    acc = None
    for k in range(width):
        term = w[k:k + 1, :] * _shifted(carry, cur, nb, k, width)
        acc = term if acc is None else acc + term
    tail = jnp.concatenate([carry, cur], axis=0)[cur.shape[0]:]
    return acc, tail


def _even_kernel(*refs, tt, nb, n_meta):
    if n_meta:
        x_ref, meta_ref, *refs = refs
    else:
        x_ref, *refs = refs
    (h0_ref, s0_ref, g_ref, win_ref, bcat_ref, lam_ref, ccat_ref, dskip_ref, wglu_ref, bglu_ref,
     wsc_ref, wout_ref, xo_ref, ho_ref, so_ref, hcar_ref, scar_ref, hb_ref) = refs
    t_idx = pl.program_id(1)
    n_t = pl.num_programs(1)
    tm = tt * nb

    @pl.when(t_idx == 0)
    def _():
        hcar_ref[...] = h0_ref[...]
        scar_ref[...] = s0_ref[...].reshape(2 * nb, D_SCONV)

    x = x_ref[...].reshape(tm, D_MODEL)
    if n_meta:
        meta = jnp.broadcast_to(meta_ref[...][:, None, :], (n_meta, nb, D_MODEL))
        first = jnp.concatenate([meta.reshape(n_meta * nb, D_MODEL), x[:tm - n_meta * nb]], axis=0)
        x = jnp.where(t_idx == 0, first, x)
    xn = _rms(x, g_ref[...]).astype(_BF16)
    u = _dot(xn, win_ref[:, :D_SSM])
    ub = u.astype(_BF16)

    y_parts = []
    for kb in range(N_KB):
        c0 = kb * 2 * KB_STATES
        hb_ref[...] = _dot(ub[:, kb * KB_COLS:(kb + 1) * KB_COLS], bcat_ref[kb])
        lam_re = jnp.broadcast_to(lam_ref[kb, 0:1, :], (SUBLANES, KB_STATES))
        lam_im = jnp.broadcast_to(lam_ref[kb, 1:2, :], (SUBLANES, KB_STATES))
        for j in range(nb // SUBLANES):
            rows = slice(j * SUBLANES, (j + 1) * SUBLANES)
            h_re0 = hcar_ref[rows, c0:c0 + KB_STATES]
            h_im0 = hcar_ref[rows, c0 + KB_STATES:c0 + 2 * KB_STATES]

            def step(t, carry, j=j, lam_re=lam_re, lam_im=lam_im):
                h_re, h_im = carry
                r = pl.ds(t * nb + j * SUBLANES, SUBLANES)
                n_re = lam_re * h_re - lam_im * h_im + hb_ref[r, 0:KB_STATES]
                n_im = lam_re * h_im + lam_im * h_re + hb_ref[r, KB_STATES:2 * KB_STATES]
                hb_ref[r, 0:KB_STATES] = n_re
                hb_ref[r, KB_STATES:2 * KB_STATES] = n_im
                return n_re, n_im

            h_re, h_im = h_re0, h_im0
            for t in range(tt):
                h_re, h_im = step(t, (h_re, h_im))
            hcar_ref[rows, c0:c0 + KB_STATES] = h_re
            hcar_ref[rows, c0 + KB_STATES:c0 + 2 * KB_STATES] = h_im
        y_parts.append(_dot(hb_ref[...].astype(_BF16), ccat_ref[kb]))
    y = jnp.concatenate(y_parts, axis=1) + dskip_ref[...] * u

    y_a = _gelu_tanh(y)
    y_a = y_a * _sigmoid(_dot(y_a.astype(_BF16), wglu_ref[...]) + bglu_ref[...])

    x_b = _dot(xn, win_ref[:, D_SSM:D_SSM + D_SCONV])
    g_c = _dot(xn, win_ref[:, D_SSM + 2 * D_SCONV:])
    g_b = _dot(xn, win_ref[:, D_SSM + D_SCONV:D_SSM + 2 * D_SCONV])
    conv, tail = _short_conv(scar_ref[...], g_c * x_b, wsc_ref[...], nb)
    scar_ref[...] = tail
    y_b = g_b * conv

    mixed = jnp.concatenate([y_a.astype(_BF16), y_b.astype(_BF16)], axis=1)
    xo_ref[...] = (x + _dot(mixed, wout_ref[...])).reshape(tt, nb, D_MODEL)

    @pl.when(t_idx == n_t - 1)
    def _():
        ho_ref[...] = hcar_ref[...]
        so_ref[...] = scar_ref[...].reshape(2, nb, D_SCONV)


def _odd_kernel(x_ref, c0_ref, g_ref, w1_ref, b1_ref, wdw_ref, lng_ref, lnb_ref, w2_ref, b2_ref,
                xo_ref, co_ref,
                win_ref, conv_ref, *, tt, nb, n_t):
    t_idx = pl.program_id(1)
    tm = tt * nb
    halo = (CONF_W - 1) * nb

    @pl.when(t_idx == 0)
    def _():
        win_ref[0:halo, :] = c0_ref[...].reshape(halo, D_CONF)

    x = x_ref[...].reshape(tm, D_MODEL)
    xn = _rms(x, g_ref[...]).astype(_BF16)
    for lo in range(0, D_CONF, PW1_SLICE):
        cols = slice(lo, lo + PW1_SLICE)
        gcols = slice(D_CONF + lo, D_CONF + lo + PW1_SLICE)
        a = _dot(xn, w1_ref[:, cols]) + b1_ref[:, cols]
        gate = _dot(xn, w1_ref[:, gcols]) + b1_ref[:, gcols]
        win_ref[halo:halo + tm, cols] = a * _sigmoid(gate)

    def conv_block(r0, n_tiles):
        view = win_ref.at[pl.ds(r0, halo + n_tiles * SUBLANES), :]
        for c in range(D_CONF // LANES):
            lanes = slice(c * LANES, (c + 1) * LANES)
            accs = [None] * n_tiles
            for k in range(CONF_W):
                tap = wdw_ref[k * SUBLANES:(k + 1) * SUBLANES, lanes]
                for j in range(n_tiles):
                    lo = k * nb + j * SUBLANES
                    term = tap * view[lo:lo + SUBLANES, lanes]
                    accs[j] = term if accs[j] is None else accs[j] + term
            for j in range(n_tiles):
                conv_ref[pl.ds(r0 + j * SUBLANES, SUBLANES), lanes] = accs[j]

    n_row_tiles = tm // SUBLANES
    block_rows = CONV_BLOCK_TILES * SUBLANES

    def conv_trip(i, _):
        conv_block(pl.multiple_of(i * block_rows, block_rows), CONV_BLOCK_TILES)
        return 0

    lax.fori_loop(0, n_row_tiles // CONV_BLOCK_TILES, conv_trip, 0)
    if n_row_tiles % CONV_BLOCK_TILES:
        conv_block(n_row_tiles // CONV_BLOCK_TILES * block_rows, n_row_tiles % CONV_BLOCK_TILES)

    conv = conv_ref[...]
    mu = jnp.mean(conv, axis=-1, keepdims=True)
    xc = conv - mu
    yv = xc * lax.rsqrt(jnp.mean(xc * xc, axis=-1, keepdims=True) + EPS)
    yv = yv * lng_ref[...] + lnb_ref[...]
    act = (yv * _sigmoid(yv)).astype(_BF16)

    out = _dot(act, w2_ref[...]) + b2_ref[...]
    xo_ref[...] = (x + out).reshape(tt, nb, D_MODEL)

    @pl.when(t_idx == n_t - 1)
    def _():
        co_ref[...] = win_ref[tm:tm + halo, :].reshape(CONF_W - 1, nb, D_CONF)

    if n_t > 1:
        assert tt >= CONF_W - 1

        @pl.when(t_idx < n_t - 1)
        def _():
            win_ref[0:halo, :] = win_ref[tm:tm + halo, :]


def _ffn_kernel(x_ref, f0_ref, g_ref, wup_ref, wc_ref, wd_ref, gfin_ref,
                xo_ref, fo_ref,
                fcar_ref, hid_ref, *, tt, nb, final_norm):
    t_idx = pl.program_id(1)
    n_t = pl.num_programs(1)
    tm = tt * nb

    @pl.when(t_idx == 0)
    def _():
        fcar_ref[...] = f0_ref[...].reshape(2 * nb, D_FF)

    x = x_ref[...].reshape(tm, D_MODEL)
    xn = _rms(x, g_ref[...]).astype(_BF16)
    for c in range(D_FF // FFN_CHUNK):
        cols = slice(c * FFN_CHUNK, (c + 1) * FFN_CHUNK)
        a = _dot(xn, wup_ref[:, cols])
        gate = _dot(xn, wup_ref[:, D_FF + c * FFN_CHUNK:D_FF + (c + 1) * FFN_CHUNK])
        a_c, tail = _short_conv(fcar_ref[:, cols], a, wc_ref[:, cols], nb)
        fcar_ref[:, cols] = tail
        hid_ref[:, cols] = ((a_c * _sigmoid(a_c)) * gate).astype(_BF16)
    y = x + _dot(hid_ref[...], wd_ref[...])
    if final_norm:
        y = _rms(y, gfin_ref[...])
    xo_ref[...] = y.reshape(tt, nb, D_MODEL)

    @pl.when(t_idx == n_t - 1)
    def _():
        fo_ref[...] = fcar_ref[...].reshape(2, nb, D_FF)


def _layer_spec(arr, layer):
    tail = (0,) * (arr.ndim - 1)
    return pl.BlockSpec((None,) + arr.shape[1:], lambda b, t: (layer,) + tail,
                        pipeline_mode=pl.Buffered(1))


def _x_spec(tt, nbt):
    return pl.BlockSpec((tt, nbt, D_MODEL), lambda b, t: (t, b, 0))


def _state_spec(steps, nbt, width):
    return pl.BlockSpec((steps, nbt, width), lambda b, t: (0, b, 0))


_PARAMS = pltpu.CompilerParams(dimension_semantics=("arbitrary", "arbitrary"),
                               vmem_limit_bytes=VMEM_LIMIT_BYTES)

_EVEN_KEYS = ("w_in", "bcat", "lam", "ccat", "dskip", "w_glu", "b_glu", "w_sconv", "w_out")
_ODD_KEYS = ("w1", "b1", "w_dw", "ln_g", "ln_b", "w2", "b2")
_FFN_KEYS = ("g_ffn", "w_up", "w_conv", "w_down")


def _even_call(x, h0, s0, w, layer, *, tt, nbt, meta=None):
    n_meta = 0 if meta is None else meta.shape[0]
    length, nb = x.shape[0] + n_meta, x.shape[1]
    grid = (nb // nbt, length // tt)
    w_specs = [_layer_spec(w["g_mix"], layer)] + [_layer_spec(w[k], layer // 2) for k in _EVEN_KEYS]
    if n_meta:
        x_specs = [pl.BlockSpec((pl.Element(tt), pl.Element(nbt), pl.Element(D_MODEL)),
                                lambda b, t: (jnp.maximum(t * tt - n_meta, 0), b * nbt, 0)),
                   pl.BlockSpec(meta.shape, lambda b, t: (0, 0))]
        x_args = (x, meta)
    else:
        x_specs, x_args = [_x_spec(tt, nbt)], (x,)
    return pl.pallas_call(
        functools.partial(_even_kernel, tt=tt, nb=nbt, n_meta=n_meta),
        grid=grid,
        in_specs=x_specs + [pl.BlockSpec((nbt, D_STATE), lambda b, t: (b, 0)),
                            _state_spec(SCONV_W - 1, nbt, D_SCONV)] + w_specs,
        out_specs=[_x_spec(tt, nbt),
                   pl.BlockSpec((nbt, D_STATE), lambda b, t: (b, 0)),
                   _state_spec(SCONV_W - 1, nbt, D_SCONV)],
        out_shape=[jax.ShapeDtypeStruct((length, nb, D_MODEL), _F32),
                   jax.ShapeDtypeStruct(h0.shape, _F32),
                   jax.ShapeDtypeStruct(s0.shape, _F32)],
        scratch_shapes=[pltpu.VMEM((nbt, D_STATE), _F32),
                        pltpu.VMEM(((SCONV_W - 1) * nbt, D_SCONV), _F32),
                        pltpu.VMEM((tt * nbt, 2 * KB_STATES), _F32)],
        compiler_params=_PARAMS,
        name="even_mixer",
    )(*x_args, h0, s0, w["g_mix"], *[w[k] for k in _EVEN_KEYS])


def _odd_call(x, c0, w, layer, *, tt, nbt):
    length, nb, _ = x.shape
    n_t = length // tt
    grid = (nb // nbt, n_t)
    w_specs = [_layer_spec(w["g_mix"], layer)] + [_layer_spec(w[k], layer // 2) for k in _ODD_KEYS]
    return pl.pallas_call(
        functools.partial(_odd_kernel, tt=tt, nb=nbt, n_t=n_t),
        grid=grid,
        in_specs=[_x_spec(tt, nbt), _state_spec(CONF_W - 1, nbt, D_CONF)] + w_specs,
        out_specs=[_x_spec(tt, nbt), _state_spec(CONF_W - 1, nbt, D_CONF)],
        out_shape=[jax.ShapeDtypeStruct(x.shape, _F32), jax.ShapeDtypeStruct(c0.shape, _F32)],
        scratch_shapes=[pltpu.VMEM(((CONF_W - 1 + tt) * nbt, D_CONF), _F32),
                        pltpu.VMEM((tt * nbt, D_CONF), _F32)],
        compiler_params=_PARAMS,
        name="odd_mixer",
    )(x, c0, w["g_mix"], *[w[k] for k in _ODD_KEYS])


def _ffn_call(x, f0, w, layer, *, tt, nbt, final_norm):
    length, nb, _ = x.shape
    grid = (nb // nbt, length // tt)
    w_specs = [_layer_spec(w[k], layer) for k in _FFN_KEYS] + [_layer_spec(w["g_final"], 0)]
    return pl.pallas_call(
        functools.partial(_ffn_kernel, tt=tt, nb=nbt, final_norm=final_norm),
        grid=grid,
        in_specs=[_x_spec(tt, nbt), _state_spec(FFN_W - 1, nbt, D_FF)] + w_specs,
        out_specs=[_x_spec(tt, nbt), _state_spec(FFN_W - 1, nbt, D_FF)],
        out_shape=[jax.ShapeDtypeStruct(x.shape, _F32), jax.ShapeDtypeStruct(f0.shape, _F32)],
        scratch_shapes=[pltpu.VMEM(((FFN_W - 1) * nbt, D_FF), _F32),
                        pltpu.VMEM((tt * nbt, D_FF), _BF16)],
        compiler_params=_PARAMS,
        name="conv_ffn",
    )(x, f0, *[w[k] for k in _FFN_KEYS], w["g_final"])


def _rows(v):
    return v.reshape(v.shape[0], 1, v.shape[-1]).astype(_F32)


def _prep_params(p):
    n_even = p["w_in_even"].shape[0]
    p_, s_ = SSM_STATE, SSM_GROUP
    lam = lax.complex(p["ssm_lam_re"].astype(_F32), p["ssm_lam_im"].astype(_F32))
    dt = jnp.exp(p["ssm_log_dt"].astype(_F32))[..., None]
    lam_bar = jnp.exp(lam * dt)
    b_bar = ((lam_bar - 1.0) / lam)[..., None] * lax.complex(
        p["ssm_b_re"].astype(_F32), p["ssm_b_im"].astype(_F32))
    eye = jnp.eye(KB_GROUPS, dtype=_F32)

    def b_block(v):
        v = v.reshape(n_even, N_KB, KB_GROUPS, p_, s_)
        return jnp.einsum("ekgpi,gh->ekgihp", v, eye).reshape(n_even, N_KB, KB_COLS, KB_STATES)

    def c_block(v):
        v = v.reshape(n_even, N_KB, KB_GROUPS, s_, p_)
        return jnp.einsum("ekgip,gh->ekgphi", v, eye).reshape(n_even, N_KB, KB_STATES, KB_COLS)

    bcat = jnp.concatenate([b_block(jnp.real(b_bar)), b_block(jnp.imag(b_bar))], axis=3)
    ccat = jnp.concatenate([c_block(p["ssm_c_re"].astype(_F32)),
                            -c_block(p["ssm_c_im"].astype(_F32))], axis=2)
    lamcat = jnp.stack([jnp.real(lam_bar).reshape(n_even, N_KB, KB_STATES),
                        jnp.imag(lam_bar).reshape(n_even, N_KB, KB_STATES)], axis=2)
    return dict(
        g_mix=_rows(p["norm_mix"]), g_ffn=_rows(p["norm_ffn"]), g_final=_rows(p["norm_final"][None]),
        w_in=p["w_in_even"].astype(_BF16), bcat=bcat.astype(_BF16), lam=lamcat,
        ccat=ccat.astype(_BF16), dskip=_rows(p["ssm_d"]), w_glu=p["w_glu"].astype(_BF16),
        b_glu=_rows(p["b_glu"]), w_sconv=p["w_sconv"].astype(_F32),
        w_out=p["w_out_even"].astype(_BF16),
        w1=p["w_conf_pw1"].astype(_BF16), b1=_rows(p["b_conf_pw1"]),
        w_dw=jnp.repeat(p["w_conf_dw"].astype(_F32), SUBLANES, axis=1),
        ln_g=_rows(p["conf_ln_g"]), ln_b=_rows(p["conf_ln_b"]),
        w2=p["w_conf_pw2"].astype(_BF16), b2=_rows(p["b_conf_pw2"]),
        w_up=p["w_ffn_up"].astype(_BF16), w_conv=p["w_ffn_conv"].astype(_F32),
        w_down=p["w_ffn_down"].astype(_BF16))


def _time_major(v):
    return jnp.transpose(v, (1, 0, 2))


def _ssm_to_flat(re, im):
    nb = re.shape[0]
    re = re.reshape(nb, N_KB, KB_STATES)
    im = im.reshape(nb, N_KB, KB_STATES)
    return jnp.concatenate([re, im], axis=2).reshape(nb, D_STATE).astype(_F32)


def _flat_to_ssm(h):
    nb = h.shape[0]
    h = h.reshape(nb, N_KB, 2, KB_STATES)
    shape = (nb, N_SSM_GROUPS, SSM_STATE)
    return h[:, :, 0].reshape(shape), h[:, :, 1].reshape(shape)


def _trunk(x, ssm_re, ssm_im, sconv, cconv, ffn_buf, w, *, tt, tt_mix, nbt, nbt_odd, meta=None):
    depth = w["g_mix"].shape[0]
    n_re, n_im, n_s, n_c, n_f = [], [], [], [], []
    for l in range(depth):
        i = l // 2
        if l % 2 == 0:
            h0 = _ssm_to_flat(ssm_re[i], ssm_im[i])
            x, h1, s1 = _even_call(x, h0, _time_major(sconv[i]), w, l, tt=tt_mix, nbt=nbt,
                                   meta=meta if l == 0 else None)
            re, im = _flat_to_ssm(h1)
            n_re.append(re)
            n_im.append(im)
            n_s.append(_time_major(s1))
        else:
            x, c1 = _odd_call(x, _time_major(cconv[i]), w, l, tt=tt_mix, nbt=nbt_odd)
            n_c.append(_time_major(c1))
        x, f1 = _ffn_call(x, _time_major(ffn_buf[l]), w, l, tt=tt, nbt=nbt, final_norm=(l == depth - 1))
        n_f.append(_time_major(f1))
    return x, jnp.stack(n_re), jnp.stack(n_im), jnp.stack(n_s), jnp.stack(n_c), jnp.stack(n_f)


def kernel(x_prompt, x_sample, state_ssm_re, state_ssm_im, state_sconv, state_cconv, state_ffn, meta_tokens, norm_mix, norm_ffn, norm_final, w_in_even, ssm_lam_re, ssm_lam_im, ssm_log_dt, ssm_b_re, ssm_b_im, ssm_c_re, ssm_c_im, ssm_d, w_glu, b_glu, w_sconv, w_out_even, w_conf_pw1, b_conf_pw1, w_conf_dw, conf_ln_g, conf_ln_b, w_conf_pw2, b_conf_pw2, w_ffn_up, w_ffn_conv, w_ffn_down):
    w = _prep_params(dict(
        norm_mix=norm_mix, norm_ffn=norm_ffn, norm_final=norm_final, w_in_even=w_in_even,
        ssm_lam_re=ssm_lam_re, ssm_lam_im=ssm_lam_im, ssm_log_dt=ssm_log_dt, ssm_b_re=ssm_b_re,
        ssm_b_im=ssm_b_im, ssm_c_re=ssm_c_re, ssm_c_im=ssm_c_im, ssm_d=ssm_d, w_glu=w_glu,
        b_glu=b_glu, w_sconv=w_sconv, w_out_even=w_out_even, w_conf_pw1=w_conf_pw1,
        b_conf_pw1=b_conf_pw1, w_conf_dw=w_conf_dw, conf_ln_g=conf_ln_g, conf_ln_b=conf_ln_b,
        w_conf_pw2=w_conf_pw2, b_conf_pw2=b_conf_pw2, w_ffn_up=w_ffn_up, w_ffn_conv=w_ffn_conv,
        w_ffn_down=w_ffn_down))
    depth = norm_mix.shape[0]
    batch, seq, _ = x_prompt.shape
    dec_batch, dec_seq, _ = x_sample.shape
    dt = x_prompt.dtype
    n_even, n_odd = (depth + 1) // 2, depth // 2

    xp = jnp.transpose(x_prompt, (1, 0, 2))
    zp_h = jnp.zeros((n_even, batch, N_SSM_GROUPS, SSM_STATE), dt)
    zp_s = jnp.zeros((n_even, batch, SCONV_W - 1, D_SCONV), dt)
    zp_c = jnp.zeros((n_odd, batch, CONF_W - 1, D_CONF), dt)
    zp_f = jnp.zeros((depth, batch, FFN_W - 1, D_FF), dt)
    yp, *p_states = _trunk(xp, zp_h, zp_h, zp_s, zp_c, zp_f, w,
                           tt=PROMPT_TILE_STEPS, tt_mix=PROMPT_MIXER_TILE_STEPS, nbt=batch, nbt_odd=batch,
                           meta=meta_tokens.astype(dt))
    y_prompt = jnp.transpose(yp[N_META:], (1, 0, 2))

    xs = jnp.transpose(x_sample, (1, 0, 2))
    ys, *s_states = _trunk(xs, state_ssm_re, state_ssm_im, state_sconv, state_cconv, state_ffn, w,
                           tt=dec_seq, tt_mix=dec_seq, nbt=dec_batch, nbt_odd=SAMPLE_ODD_BATCH_TILE)
    y_sample = jnp.transpose(ys, (1, 0, 2))

    return (y_prompt, y_sample, *p_states, *s_states)
```

```python
import functools
import math

import jax
import jax.numpy as jnp
from jax import lax
from jax.experimental import pallas as pl
from jax.experimental.pallas import tpu as pltpu

D_MODEL = 1024
N_META = 16
SSM_GROUP = 16
SSM_STATE = 64
D_SSM = D_MODEL // 2
N_SSM_GROUPS = D_SSM // SSM_GROUP
D_SCONV = D_MODEL - D_SSM
SCONV_W = 3
D_IN_EVEN = D_SSM + 3 * D_SCONV
D_CONF = D_MODEL
CONF_W = 31
D_FF = ((8 * D_MODEL // 3 + 127) // 128) * 128
FFN_W = 3
EPS = 1e-6

LANES = 128
SUBLANES = 8
KB_COLS = LANES
N_KB = D_SSM // KB_COLS
KB_GROUPS = KB_COLS // SSM_GROUP
KB_STATES = KB_GROUPS * SSM_STATE
D_STATE = N_KB * 2 * KB_STATES
FFN_CHUNK = 256
PW1_SLICE = 256
VMEM_LIMIT_BYTES = 56 * 1024 * 1024
PROMPT_TILE_STEPS = 86
PROMPT_MIXER_TILE_STEPS = 129
SAMPLE_ODD_BATCH_TILE = 32
CONV_BLOCK_TILES = 8

_F32 = jnp.float32
_BF16 = jnp.bfloat16


def _dot(a, b):
    return jnp.dot(a, b, preferred_element_type=_F32)


def _rms(x, g):
    return (x * lax.rsqrt(jnp.mean(x * x, axis=-1, keepdims=True) + EPS)) * g


def _sigmoid(x):
    return 1.0 / (1.0 + jnp.exp(-x))


def _gelu_tanh(x):
    c = math.sqrt(2.0 / math.pi)
    return x * (0.5 * (1.0 + jnp.tanh(c * (x + 0.044715 * (x * x * x)))))


def _shifted(carry, cur, nb, k, width):
    back = (width - 1 - k) * nb
    if back == 0:
        return cur
    tm = cur.shape[0]
    head = carry[carry.shape[0] - back:]
    if back >= tm:
        return head[:tm]
    return jnp.concatenate([head, cur[:tm - back]], axis=0)


def _short_conv(carry, cur, w, nb):
    width = w.shape[0]
    acc = None
    for k in range(width):
        term = w[k:k + 1, :] * _shifted(carry, cur, nb, k, width)
        acc = term if acc is None else acc + term
    tail = jnp.concatenate([carry, cur], axis=0)[cur.shape[0]:]
    return acc, tail


def _even_kernel(*refs, tt, nb, n_meta):
    if n_meta:
        x_ref, meta_ref, *refs = refs
    else:
        x_ref, *refs = refs
    (h0_ref, s0_ref, g_ref, win_ref, bcat_ref, lam_ref, ccat_ref, dskip_ref, wglu_ref, bglu_ref,
     wsc_ref, wout_ref, xo_ref, ho_ref, so_ref, hcar_ref, scar_ref, hb_ref) = refs
    t_idx = pl.program_id(1)
    n_t = pl.num_programs(1)
    tm = tt * nb

    @pl.when(t_idx == 0)
    def _():
        hcar_ref[...] = h0_ref[...]
        for k in range(SCONV_W - 1):
            scar_ref[k * nb:(k + 1) * nb, :] = s0_ref[:, k * D_SCONV:(k + 1) * D_SCONV]

    x = x_ref[...].reshape(tm, D_MODEL)
    if n_meta:
        meta = jnp.broadcast_to(meta_ref[...][:, None, :], (n_meta, nb, D_MODEL))
        first = jnp.concatenate([meta.reshape(n_meta * nb, D_MODEL), x[:tm - n_meta * nb]], axis=0)
        x = jnp.where(t_idx == 0, first, x)
    xn = _rms(x, g_ref[...]).astype(_BF16)
    u = _dot(xn, win_ref[:, :D_SSM])
    ub = u.astype(_BF16)

    y_parts = []
    for kb in range(N_KB):
        c0 = kb * 2 * KB_STATES
        hb_ref[...] = _dot(ub[:, kb * KB_COLS:(kb + 1) * KB_COLS], bcat_ref[kb])
        lam_re = jnp.broadcast_to(lam_ref[kb, 0:1, :], (SUBLANES, KB_STATES))
        lam_im = jnp.broadcast_to(lam_ref[kb, 1:2, :], (SUBLANES, KB_STATES))
        for j in range(nb // SUBLANES):
            rows = slice(j * SUBLANES, (j + 1) * SUBLANES)
            h_re0 = hcar_ref[rows, c0:c0 + KB_STATES]
            h_im0 = hcar_ref[rows, c0 + KB_STATES:c0 + 2 * KB_STATES]

            def step(t, carry, j=j, lam_re=lam_re, lam_im=lam_im):
                h_re, h_im = carry
                r = pl.ds(t * nb + j * SUBLANES, SUBLANES)
                n_re = lam_re * h_re - lam_im * h_im + hb_ref[r, 0:KB_STATES]
                n_im = lam_re * h_im + lam_im * h_re + hb_ref[r, KB_STATES:2 * KB_STATES]
                hb_ref[r, 0:KB_STATES] = n_re
                hb_ref[r, KB_STATES:2 * KB_STATES] = n_im
                return n_re, n_im

            h_re, h_im = h_re0, h_im0
            for t in range(tt):
                h_re, h_im = step(t, (h_re, h_im))
            hcar_ref[rows, c0:c0 + KB_STATES] = h_re
            hcar_ref[rows, c0 + KB_STATES:c0 + 2 * KB_STATES] = h_im
        y_parts.append(_dot(hb_ref[...].astype(_BF16), ccat_ref[kb]))
    y = jnp.concatenate(y_parts, axis=1) + dskip_ref[...] * u

    y_a = _gelu_tanh(y)
    y_a = y_a * _sigmoid(_dot(y_a.astype(_BF16), wglu_ref[...]) + bglu_ref[...])

    x_b = _dot(xn, win_ref[:, D_SSM:D_SSM + D_SCONV])
    g_c = _dot(xn, win_ref[:, D_SSM + 2 * D_SCONV:])
    g_b = _dot(xn, win_ref[:, D_SSM + D_SCONV:D_SSM + 2 * D_SCONV])
    conv, tail = _short_conv(scar_ref[...], g_c * x_b, wsc_ref[...], nb)
    scar_ref[...] = tail
    y_b = g_b * conv

    mixed = jnp.concatenate([y_a.astype(_BF16), y_b.astype(_BF16)], axis=1)
    xo_ref[...] = (x + _dot(mixed, wout_ref[...])).reshape(tt, nb, D_MODEL)

    @pl.when(t_idx == n_t - 1)
    def _():
        ho_ref[...] = hcar_ref[...]
        for k in range(SCONV_W - 1):
            so_ref[:, k * D_SCONV:(k + 1) * D_SCONV] = scar_ref[k * nb:(k + 1) * nb, :]


def _odd_kernel(x_ref, c0_ref, g_ref, w1_ref, b1_ref, wdw_ref, lng_ref, lnb_ref, w2_ref, b2_ref,
                xo_ref, co_ref,
                win_ref, conv_ref, *, tt, nb, n_t):
    t_idx = pl.program_id(1)
    tm = tt * nb
    halo = (CONF_W - 1) * nb

    @pl.when(t_idx == 0)
    def _():
        win_ref[0:halo, :] = c0_ref[...].reshape(halo, D_CONF)

    x = x_ref[...].reshape(tm, D_MODEL)
    xn = _rms(x, g_ref[...]).astype(_BF16)
    for lo in range(0, D_CONF, PW1_SLICE):
        cols = slice(lo, lo + PW1_SLICE)
        gcols = slice(D_CONF + lo, D_CONF + lo + PW1_SLICE)
        a = _dot(xn, w1_ref[:, cols]) + b1_ref[:, cols]
        gate = _dot(xn, w1_ref[:, gcols]) + b1_ref[:, gcols]
        win_ref[halo:halo + tm, cols] = a * _sigmoid(gate)

    def conv_block(r0, n_tiles):
        view = win_ref.at[pl.ds(r0, halo + n_tiles * SUBLANES), :]
        for c in range(D_CONF // LANES):
            lanes = slice(c * LANES, (c + 1) * LANES)
            accs = [None] * n_tiles
            for k in range(CONF_W):
                tap = wdw_ref[k * SUBLANES:(k + 1) * SUBLANES, lanes]
                for j in range(n_tiles):
                    lo = k * nb + j * SUBLANES
                    term = tap * view[lo:lo + SUBLANES, lanes]
                    accs[j] = term if accs[j] is None else accs[j] + term
            for j in range(n_tiles):
                conv_ref[pl.ds(r0 + j * SUBLANES, SUBLANES), lanes] = accs[j]

    n_row_tiles = tm // SUBLANES
    block_rows = CONV_BLOCK_TILES * SUBLANES

    def conv_trip(i, _):
        conv_block(pl.multiple_of(i * block_rows, block_rows), CONV_BLOCK_TILES)
        return 0

    lax.fori_loop(0, n_row_tiles // CONV_BLOCK_TILES, conv_trip, 0)
    if n_row_tiles % CONV_BLOCK_TILES:
        conv_block(n_row_tiles // CONV_BLOCK_TILES * block_rows, n_row_tiles % CONV_BLOCK_TILES)

    conv = conv_ref[...]
    mu = jnp.mean(conv, axis=-1, keepdims=True)
    xc = conv - mu
    yv = xc * lax.rsqrt(jnp.mean(xc * xc, axis=-1, keepdims=True) + EPS)
    yv = yv * lng_ref[...] + lnb_ref[...]
    act = (yv * _sigmoid(yv)).astype(_BF16)

    out = _dot(act, w2_ref[...]) + b2_ref[...]
    xo_ref[...] = (x + out).reshape(tt, nb, D_MODEL)

    @pl.when(t_idx == n_t - 1)
    def _():
        co_ref[...] = win_ref[tm:tm + halo, :].reshape(CONF_W - 1, nb, D_CONF)

    if n_t > 1:
        assert tt >= CONF_W - 1

        @pl.when(t_idx < n_t - 1)
        def _():
            win_ref[0:halo, :] = win_ref[tm:tm + halo, :]


def _ffn_kernel(x_ref, f0_ref, g_ref, wup_ref, wc_ref, wd_ref, gfin_ref,
                xo_ref, fo_ref,
                fcar_ref, hid_ref, *, tt, nb, final_norm):
    t_idx = pl.program_id(1)
    n_t = pl.num_programs(1)
    tm = tt * nb

    @pl.when(t_idx == 0)
    def _():
        fcar_ref[...] = f0_ref[...].reshape(2 * nb, D_FF)

    x = x_ref[...].reshape(tm, D_MODEL)
    xn = _rms(x, g_ref[...]).astype(_BF16)
    for c in range(D_FF // FFN_CHUNK):
        cols = slice(c * FFN_CHUNK, (c + 1) * FFN_CHUNK)
        a = _dot(xn, wup_ref[:, cols])
        gate = _dot(xn, wup_ref[:, D_FF + c * FFN_CHUNK:D_FF + (c + 1) * FFN_CHUNK])
        a_c, tail = _short_conv(fcar_ref[:, cols], a, wc_ref[:, cols], nb)
        fcar_ref[:, cols] = tail
        hid_ref[:, cols] = ((a_c * _sigmoid(a_c)) * gate).astype(_BF16)
    y = x + _dot(hid_ref[...], wd_ref[...])
    if final_norm:
        y = _rms(y, gfin_ref[...])
    xo_ref[...] = y.reshape(tt, nb, D_MODEL)

    @pl.when(t_idx == n_t - 1)
    def _():
        fo_ref[...] = fcar_ref[...].reshape(2, nb, D_FF)


def _layer_spec(arr, layer):
    tail = (0,) * (arr.ndim - 1)
    return pl.BlockSpec((None,) + arr.shape[1:], lambda b, t: (layer,) + tail,
                        pipeline_mode=pl.Buffered(1))


def _x_spec(tt, nbt):
    return pl.BlockSpec((tt, nbt, D_MODEL), lambda b, t: (t, b, 0))


def _state_spec(steps, nbt, width):
    return pl.BlockSpec((steps, nbt, width), lambda b, t: (0, b, 0))


_PARAMS = pltpu.CompilerParams(dimension_semantics=("arbitrary", "arbitrary"),
                               vmem_limit_bytes=VMEM_LIMIT_BYTES)

_EVEN_KEYS = ("w_in", "bcat", "lam", "ccat", "dskip", "w_glu", "b_glu", "w_sconv", "w_out")
_ODD_KEYS = ("w1", "b1", "w_dw", "ln_g", "ln_b", "w2", "b2")
_FFN_KEYS = ("g_ffn", "w_up", "w_conv", "w_down")


def _even_call(x, h0, s0, w, layer, *, tt, nbt, meta=None):
    n_meta = 0 if meta is None else meta.shape[0]
    length, nb = x.shape[0] + n_meta, x.shape[1]
    grid = (nb // nbt, length // tt)
    w_specs = [_layer_spec(w["g_mix"], layer)] + [_layer_spec(w[k], layer // 2) for k in _EVEN_KEYS]
    if n_meta:
        x_specs = [pl.BlockSpec((pl.Element(tt), pl.Element(nbt), pl.Element(D_MODEL)),
                                lambda b, t: (jnp.maximum(t * tt - n_meta, 0), b * nbt, 0)),
                   pl.BlockSpec(meta.shape, lambda b, t: (0, 0))]
        x_args = (x, meta)
    else:
        x_specs, x_args = [_x_spec(tt, nbt)], (x,)
    return pl.pallas_call(
        functools.partial(_even_kernel, tt=tt, nb=nbt, n_meta=n_meta),
        grid=grid,
        in_specs=x_specs + [pl.BlockSpec((nbt, D_STATE), lambda b, t: (b, 0)),
                            pl.BlockSpec((nbt, (SCONV_W - 1) * D_SCONV), lambda b, t: (b, 0))] + w_specs,
        out_specs=[_x_spec(tt, nbt),
                   pl.BlockSpec((nbt, D_STATE), lambda b, t: (b, 0)),
                   pl.BlockSpec((nbt, (SCONV_W - 1) * D_SCONV), lambda b, t: (b, 0))],
        out_shape=[jax.ShapeDtypeStruct((length, nb, D_MODEL), _F32),
                   jax.ShapeDtypeStruct(h0.shape, _F32),
                   jax.ShapeDtypeStruct(s0.shape, _F32)],
        scratch_shapes=[pltpu.VMEM((nbt, D_STATE), _F32),
                        pltpu.VMEM(((SCONV_W - 1) * nbt, D_SCONV), _F32),
                        pltpu.VMEM((tt * nbt, 2 * KB_STATES), _F32)],
        compiler_params=_PARAMS,
        name="even_mixer",
    )(*x_args, h0, s0, w["g_mix"], *[w[k] for k in _EVEN_KEYS])


def _odd_call(x, c0, w, layer, *, tt, nbt):
    length, nb, _ = x.shape
    n_t = length // tt
    grid = (nb // nbt, n_t)
    w_specs = [_layer_spec(w["g_mix"], layer)] + [_layer_spec(w[k], layer // 2) for k in _ODD_KEYS]
    return pl.pallas_call(
        functools.partial(_odd_kernel, tt=tt, nb=nbt, n_t=n_t),
        grid=grid,
        in_specs=[_x_spec(tt, nbt), _state_spec(CONF_W - 1, nbt, D_CONF)] + w_specs,
        out_specs=[_x_spec(tt, nbt), _state_spec(CONF_W - 1, nbt, D_CONF)],
        out_shape=[jax.ShapeDtypeStruct(x.shape, _F32), jax.ShapeDtypeStruct(c0.shape, _F32)],
        scratch_shapes=[pltpu.VMEM(((CONF_W - 1 + tt) * nbt, D_CONF), _F32),
                        pltpu.VMEM((tt * nbt, D_CONF), _F32)],
        compiler_params=_PARAMS,
        name="odd_mixer",
    )(x, c0, w["g_mix"], *[w[k] for k in _ODD_KEYS])


def _ffn_call(x, f0, w, layer, *, tt, nbt, final_norm):
    length, nb, _ = x.shape
    grid = (nb // nbt, length // tt)
    w_specs = [_layer_spec(w[k], layer) for k in _FFN_KEYS] + [_layer_spec(w["g_final"], 0)]
    return pl.pallas_call(
        functools.partial(_ffn_kernel, tt=tt, nb=nbt, final_norm=final_norm),
        grid=grid,
        in_specs=[_x_spec(tt, nbt), _state_spec(FFN_W - 1, nbt, D_FF)] + w_specs,
        out_specs=[_x_spec(tt, nbt), _state_spec(FFN_W - 1, nbt, D_FF)],
        out_shape=[jax.ShapeDtypeStruct(x.shape, _F32), jax.ShapeDtypeStruct(f0.shape, _F32)],
        scratch_shapes=[pltpu.VMEM(((FFN_W - 1) * nbt, D_FF), _F32),
                        pltpu.VMEM((tt * nbt, D_FF), _BF16)],
        compiler_params=_PARAMS,
        name="conv_ffn",
    )(x, f0, *[w[k] for k in _FFN_KEYS], w["g_final"])


def _rows(v):
    return v.reshape(v.shape[0], 1, v.shape[-1]).astype(_F32)


def _prep_params(p):
    n_even = p["w_in_even"].shape[0]
    p_, s_ = SSM_STATE, SSM_GROUP
    lam = lax.complex(p["ssm_lam_re"].astype(_F32), p["ssm_lam_im"].astype(_F32))
    dt = jnp.exp(p["ssm_log_dt"].astype(_F32))[..., None]
    lam_bar = jnp.exp(lam * dt)
    b_bar = ((lam_bar - 1.0) / lam)[..., None] * lax.complex(
        p["ssm_b_re"].astype(_F32), p["ssm_b_im"].astype(_F32))
    eye = jnp.eye(KB_GROUPS, dtype=_F32)

    def b_block(v):
        v = v.reshape(n_even, N_KB, KB_GROUPS, p_, s_)
        return jnp.einsum("ekgpi,gh->ekgihp", v, eye).reshape(n_even, N_KB, KB_COLS, KB_STATES)

    def c_block(v):
        v = v.reshape(n_even, N_KB, KB_GROUPS, s_, p_)
        return jnp.einsum("ekgip,gh->ekgphi", v, eye).reshape(n_even, N_KB, KB_STATES, KB_COLS)

    bcat = jnp.concatenate([b_block(jnp.real(b_bar)), b_block(jnp.imag(b_bar))], axis=3)
    ccat = jnp.concatenate([c_block(p["ssm_c_re"].astype(_F32)),
                            -c_block(p["ssm_c_im"].astype(_F32))], axis=2)
    lamcat = jnp.stack([jnp.real(lam_bar).reshape(n_even, N_KB, KB_STATES),
                        jnp.imag(lam_bar).reshape(n_even, N_KB, KB_STATES)], axis=2)
    return dict(
        g_mix=_rows(p["norm_mix"]), g_ffn=_rows(p["norm_ffn"]), g_final=_rows(p["norm_final"][None]),
        w_in=p["w_in_even"].astype(_BF16), bcat=bcat.astype(_BF16), lam=lamcat,
        ccat=ccat.astype(_BF16), dskip=_rows(p["ssm_d"]), w_glu=p["w_glu"].astype(_BF16),
        b_glu=_rows(p["b_glu"]), w_sconv=p["w_sconv"].astype(_F32),
        w_out=p["w_out_even"].astype(_BF16),
        w1=p["w_conf_pw1"].astype(_BF16), b1=_rows(p["b_conf_pw1"]),
        w_dw=jnp.repeat(p["w_conf_dw"].astype(_F32), SUBLANES, axis=1),
        ln_g=_rows(p["conf_ln_g"]), ln_b=_rows(p["conf_ln_b"]),
        w2=p["w_conf_pw2"].astype(_BF16), b2=_rows(p["b_conf_pw2"]),
        w_up=p["w_ffn_up"].astype(_BF16), w_conv=p["w_ffn_conv"].astype(_F32),
        w_down=p["w_ffn_down"].astype(_BF16))


def _time_major(v):
    return jnp.transpose(v, (1, 0, 2))


def _ssm_to_flat(re, im):
    nb = re.shape[0]
    re = re.reshape(nb, N_KB, KB_STATES)
    im = im.reshape(nb, N_KB, KB_STATES)
    return jnp.concatenate([re, im], axis=2).reshape(nb, D_STATE).astype(_F32)


def _flat_to_ssm(h):
    nb = h.shape[0]
    h = h.reshape(nb, N_KB, 2, KB_STATES)
    shape = (nb, N_SSM_GROUPS, SSM_STATE)
    return h[:, :, 0].reshape(shape), h[:, :, 1].reshape(shape)


def _trunk(x, ssm_re, ssm_im, sconv, cconv, ffn_buf, w, *, tt, tt_mix, nbt, nbt_odd, meta=None):
    depth = w["g_mix"].shape[0]
    n_re, n_im, n_s, n_c, n_f = [], [], [], [], []
    for l in range(depth):
        i = l // 2
        if l % 2 == 0:
            h0 = _ssm_to_flat(ssm_re[i], ssm_im[i])
            s0 = sconv[i].reshape(sconv[i].shape[0], -1)
            x, h1, s1 = _even_call(x, h0, s0, w, l, tt=tt_mix, nbt=nbt,
                                   meta=meta if l == 0 else None)
            re, im = _flat_to_ssm(h1)
            n_re.append(re)
            n_im.append(im)
            n_s.append(s1.reshape(sconv[i].shape))
        else:
            x, c1 = _odd_call(x, _time_major(cconv[i]), w, l, tt=tt_mix, nbt=nbt_odd)
            n_c.append(_time_major(c1))
        x, f1 = _ffn_call(x, _time_major(ffn_buf[l]), w, l, tt=tt, nbt=nbt, final_norm=(l == depth - 1))
        n_f.append(_time_major(f1))
    return x, jnp.stack(n_re), jnp.stack(n_im), jnp.stack(n_s), jnp.stack(n_c), jnp.stack(n_f)


def kernel(x_prompt, x_sample, state_ssm_re, state_ssm_im, state_sconv, state_cconv, state_ffn, meta_tokens, norm_mix, norm_ffn, norm_final, w_in_even, ssm_lam_re, ssm_lam_im, ssm_log_dt, ssm_b_re, ssm_b_im, ssm_c_re, ssm_c_im, ssm_d, w_glu, b_glu, w_sconv, w_out_even, w_conf_pw1, b_conf_pw1, w_conf_dw, conf_ln_g, conf_ln_b, w_conf_pw2, b_conf_pw2, w_ffn_up, w_ffn_conv, w_ffn_down):
    w = _prep_params(dict(
        norm_mix=norm_mix, norm_ffn=norm_ffn, norm_final=norm_final, w_in_even=w_in_even,
        ssm_lam_re=ssm_lam_re, ssm_lam_im=ssm_lam_im, ssm_log_dt=ssm_log_dt, ssm_b_re=ssm_b_re,
        ssm_b_im=ssm_b_im, ssm_c_re=ssm_c_re, ssm_c_im=ssm_c_im, ssm_d=ssm_d, w_glu=w_glu,
        b_glu=b_glu, w_sconv=w_sconv, w_out_even=w_out_even, w_conf_pw1=w_conf_pw1,
        b_conf_pw1=b_conf_pw1, w_conf_dw=w_conf_dw, conf_ln_g=conf_ln_g, conf_ln_b=conf_ln_b,
        w_conf_pw2=w_conf_pw2, b_conf_pw2=b_conf_pw2, w_ffn_up=w_ffn_up, w_ffn_conv=w_ffn_conv,
        w_ffn_down=w_ffn_down))
    depth = norm_mix.shape[0]
    batch, seq, _ = x_prompt.shape
    dec_batch, dec_seq, _ = x_sample.shape
    dt = x_prompt.dtype
    n_even, n_odd = (depth + 1) // 2, depth // 2

    xp = jnp.transpose(x_prompt, (1, 0, 2))
    zp_h = jnp.zeros((n_even, batch, N_SSM_GROUPS, SSM_STATE), dt)
    zp_s = jnp.zeros((n_even, batch, SCONV_W - 1, D_SCONV), dt)
    zp_c = jnp.zeros((n_odd, batch, CONF_W - 1, D_CONF), dt)
    zp_f = jnp.zeros((depth, batch, FFN_W - 1, D_FF), dt)
    yp, *p_states = _trunk(xp, zp_h, zp_h, zp_s, zp_c, zp_f, w,
                           tt=PROMPT_TILE_STEPS, tt_mix=PROMPT_MIXER_TILE_STEPS, nbt=batch, nbt_odd=batch,
                           meta=meta_tokens.astype(dt))
    y_prompt = jnp.transpose(yp[N_META:], (1, 0, 2))

    xs = jnp.transpose(x_sample, (1, 0, 2))
    ys, *s_states = _trunk(xs, state_ssm_re, state_ssm_im, state_sconv, state_cconv, state_ffn, w,
                           tt=dec_seq, tt_mix=dec_seq, nbt=dec_batch, nbt_odd=SAMPLE_ODD_BATCH_TILE)
    y_sample = jnp.transpose(ys, (1, 0, 2))

    return (y_prompt, y_sample, *p_states, *s_states)
```

```python
import functools
import math

import jax
import jax.numpy as jnp
from jax import lax
from jax.experimental import pallas as pl
from jax.experimental.pallas import tpu as pltpu

D_MODEL = 1024
N_META = 16
SSM_GROUP = 16
SSM_STATE = 64
D_SSM = D_MODEL // 2
N_SSM_GROUPS = D_SSM // SSM_GROUP
D_SCONV = D_MODEL - D_SSM
SCONV_W = 3
D_IN_EVEN = D_SSM + 3 * D_SCONV
D_CONF = D_MODEL
CONF_W = 31
D_FF = ((8 * D_MODEL // 3 + 127) // 128) * 128
FFN_W = 3
EPS = 1e-6

LANES = 128
SUBLANES = 8
KB_COLS = LANES
N_KB = D_SSM // KB_COLS
KB_GROUPS = KB_COLS // SSM_GROUP
KB_STATES = KB_GROUPS * SSM_STATE
D_STATE = N_KB * 2 * KB_STATES
FFN_CHUNK = 256
PW1_SLICE = 256
VMEM_LIMIT_BYTES = 56 * 1024 * 1024
PROMPT_TILE_STEPS = 129
PROMPT_MIXER_TILE_STEPS = 129
SAMPLE_ODD_BATCH_TILE = 32
CONV_BLOCK_TILES = 8

_F32 = jnp.float32
_BF16 = jnp.bfloat16


def _dot(a, b):
    return jnp.dot(a, b, preferred_element_type=_F32)


def _rms(x, g):
    return (x * lax.rsqrt(jnp.mean(x * x, axis=-1, keepdims=True) + EPS)) * g


def _sigmoid(x):
    return 1.0 / (1.0 + jnp.exp(-x))


def _gelu_tanh(x):
    c = math.sqrt(2.0 / math.pi)
    return x * (0.5 * (1.0 + jnp.tanh(c * (x + 0.044715 * (x * x * x)))))


def _shifted(carry, cur, nb, k, width):
    back = (width - 1 - k) * nb
    if back == 0:
        return cur
    tm = cur.shape[0]
    head = carry[carry.shape[0] - back:]
    if back >= tm:
        return head[:tm]
    return jnp.concatenate([head, cur[:tm - back]], axis=0)


def _short_conv(carry, cur, w, nb):
    width = w.shape[0]
    acc = None
    for k in range(width):
        term = w[k:k + 1, :] * _shifted(carry, cur, nb, k, width)
        acc = term if acc is None else acc + term
    tail = jnp.concatenate([carry, cur], axis=0)[cur.shape[0]:]
    return acc, tail


def _even_kernel(*refs, tt, nb, n_meta):
    if n_meta:
        x_ref, meta_ref, *refs = refs
    else:
        x_ref, *refs = refs
    (h0_ref, s0_ref, g_ref, win_ref, bcat_ref, lam_ref, ccat_ref, dskip_ref, wglu_ref, bglu_ref,
     wsc_ref, wout_ref, xo_ref, ho_ref, so_ref, hcar_ref, scar_ref, hb_ref) = refs
    t_idx = pl.program_id(1)
    n_t = pl.num_programs(1)
    tm = tt * nb

    @pl.when(t_idx == 0)
    def _():
        hcar_ref[...] = h0_ref[...]
        scar_ref[...] = s0_ref[...].reshape(2 * nb, D_SCONV)

    x = x_ref[...].reshape(tm, D_MODEL)
    if n_meta:
        meta = jnp.broadcast_to(meta_ref[...][:, None, :], (n_meta, nb, D_MODEL))
        first = jnp.concatenate([meta.reshape(n_meta * nb, D_MODEL), x[:tm - n_meta * nb]], axis=0)
        x = jnp.where(t_idx == 0, first, x)
    xn = _rms(x, g_ref[...]).astype(_BF16)
    u = _dot(xn, win_ref[:, :D_SSM])
    ub = u.astype(_BF16)

    y_parts = []
    for kb in range(N_KB):
        c0 = kb * 2 * KB_STATES
        hb_ref[...] = _dot(ub[:, kb * KB_COLS:(kb + 1) * KB_COLS], bcat_ref[kb])
        lam_re = jnp.broadcast_to(lam_ref[kb, 0:1, :], (SUBLANES, KB_STATES))
        lam_im = jnp.broadcast_to(lam_ref[kb, 1:2, :], (SUBLANES, KB_STATES))
        for j in range(nb // SUBLANES):
            rows = slice(j * SUBLANES, (j + 1) * SUBLANES)
            h_re0 = hcar_ref[rows, c0:c0 + KB_STATES]
            h_im0 = hcar_ref[rows, c0 + KB_STATES:c0 + 2 * KB_STATES]

            def step(t, carry, j=j, lam_re=lam_re, lam_im=lam_im):
                h_re, h_im = carry
                r = pl.ds(t * nb + j * SUBLANES, SUBLANES)
                n_re = lam_re * h_re - lam_im * h_im + hb_ref[r, 0:KB_STATES]
                n_im = lam_re * h_im + lam_im * h_re + hb_ref[r, KB_STATES:2 * KB_STATES]
                hb_ref[r, 0:KB_STATES] = n_re
                hb_ref[r, KB_STATES:2 * KB_STATES] = n_im
                return n_re, n_im

            h_re, h_im = h_re0, h_im0
            for t in range(tt):
                h_re, h_im = step(t, (h_re, h_im))
            hcar_ref[rows, c0:c0 + KB_STATES] = h_re
            hcar_ref[rows, c0 + KB_STATES:c0 + 2 * KB_STATES] = h_im
        y_parts.append(_dot(hb_ref[...].astype(_BF16), ccat_ref[kb]))
    y = jnp.concatenate(y_parts, axis=1) + dskip_ref[...] * u

    y_a = _gelu_tanh(y)
    y_a = y_a * _sigmoid(_dot(y_a.astype(_BF16), wglu_ref[...]) + bglu_ref[...])

    x_b = _dot(xn, win_ref[:, D_SSM:D_SSM + D_SCONV])
    g_c = _dot(xn, win_ref[:, D_SSM + 2 * D_SCONV:])
    g_b = _dot(xn, win_ref[:, D_SSM + D_SCONV:D_SSM + 2 * D_SCONV])
    conv, tail = _short_conv(scar_ref[...], g_c * x_b, wsc_ref[...], nb)
    scar_ref[...] = tail
    y_b = g_b * conv

    mixed = jnp.concatenate([y_a.astype(_BF16), y_b.astype(_BF16)], axis=1)
    xo_ref[...] = (x + _dot(mixed, wout_ref[...])).reshape(tt, nb, D_MODEL)

    @pl.when(t_idx == n_t - 1)
    def _():
        ho_ref[...] = hcar_ref[...]
        so_ref[...] = scar_ref[...].reshape(2, nb, D_SCONV)


def _odd_kernel(x_ref, c0_ref, g_ref, w1_ref, b1_ref, wdw_ref, lng_ref, lnb_ref, w2_ref, b2_ref,
                xo_ref, co_ref,
                win_ref, conv_ref, *, tt, nb, n_t):
    t_idx = pl.program_id(1)
    tm = tt * nb
    halo = (CONF_W - 1) * nb

    @pl.when(t_idx == 0)
    def _():
        win_ref[0:halo, :] = c0_ref[...].reshape(halo, D_CONF)

    x = x_ref[...].reshape(tm, D_MODEL)
    xn = _rms(x, g_ref[...]).astype(_BF16)
    for lo in range(0, D_CONF, PW1_SLICE):
        cols = slice(lo, lo + PW1_SLICE)
        gcols = slice(D_CONF + lo, D_CONF + lo + PW1_SLICE)
        a = _dot(xn, w1_ref[:, cols]) + b1_ref[:, cols]
        gate = _dot(xn, w1_ref[:, gcols]) + b1_ref[:, gcols]
        win_ref[halo:halo + tm, cols] = a * _sigmoid(gate)

    def conv_block(r0, n_tiles):
        view = win_ref.at[pl.ds(r0, halo + n_tiles * SUBLANES), :]
        for c in range(D_CONF // LANES):
            lanes = slice(c * LANES, (c + 1) * LANES)
            accs = [None] * n_tiles
            for k in range(CONF_W):
                tap = wdw_ref[k * SUBLANES:(k + 1) * SUBLANES, lanes]
                for j in range(n_tiles):
                    lo = k * nb + j * SUBLANES
                    term = tap * view[lo:lo + SUBLANES, lanes]
                    accs[j] = term if accs[j] is None else accs[j] + term
            for j in range(n_tiles):
                conv_ref[pl.ds(r0 + j * SUBLANES, SUBLANES), lanes] = accs[j]

    n_row_tiles = tm // SUBLANES
    block_rows = CONV_BLOCK_TILES * SUBLANES

    def conv_trip(i, _):
        conv_block(pl.multiple_of(i * block_rows, block_rows), CONV_BLOCK_TILES)
        return 0

    lax.fori_loop(0, n_row_tiles // CONV_BLOCK_TILES, conv_trip, 0)
    if n_row_tiles % CONV_BLOCK_TILES:
        conv_block(n_row_tiles // CONV_BLOCK_TILES * block_rows, n_row_tiles % CONV_BLOCK_TILES)

    conv = conv_ref[...]
    mu = jnp.mean(conv, axis=-1, keepdims=True)
    xc = conv - mu
    yv = xc * lax.rsqrt(jnp.mean(xc * xc, axis=-1, keepdims=True) + EPS)
    yv = yv * lng_ref[...] + lnb_ref[...]
    act = (yv * _sigmoid(yv)).astype(_BF16)

    out = _dot(act, w2_ref[...]) + b2_ref[...]
    xo_ref[...] = (x + out).reshape(tt, nb, D_MODEL)

    @pl.when(t_idx == n_t - 1)
    def _():
        co_ref[...] = win_ref[tm:tm + halo, :].reshape(CONF_W - 1, nb, D_CONF)

    if n_t > 1:
        assert tt >= CONF_W - 1

        @pl.when(t_idx < n_t - 1)
        def _():
            win_ref[0:halo, :] = win_ref[tm:tm + halo, :]


def _ffn_kernel(x_ref, f0_ref, g_ref, wup_ref, wc_ref, wd_ref, gfin_ref,
                xo_ref, fo_ref,
                fcar_ref, hid_ref, *, tt, nb, final_norm):
    t_idx = pl.program_id(1)
    n_t = pl.num_programs(1)
    tm = tt * nb

    @pl.when(t_idx == 0)
    def _():
        fcar_ref[...] = f0_ref[...].reshape(2 * nb, D_FF)

    x = x_ref[...].reshape(tm, D_MODEL)
    xn = _rms(x, g_ref[...]).astype(_BF16)
    for c in range(D_FF // FFN_CHUNK):
        cols = slice(c * FFN_CHUNK, (c + 1) * FFN_CHUNK)
        a = _dot(xn, wup_ref[:, cols])
        gate = _dot(xn, wup_ref[:, D_FF + c * FFN_CHUNK:D_FF + (c + 1) * FFN_CHUNK])
        a_c, tail = _short_conv(fcar_ref[:, cols], a, wc_ref[:, cols], nb)
        fcar_ref[:, cols] = tail
        hid_ref[:, cols] = ((a_c * _sigmoid(a_c)) * gate).astype(_BF16)
    y = x + _dot(hid_ref[...], wd_ref[...])
    if final_norm:
        y = _rms(y, gfin_ref[...])
    xo_ref[...] = y.reshape(tt, nb, D_MODEL)

    @pl.when(t_idx == n_t - 1)
    def _():
        fo_ref[...] = fcar_ref[...].reshape(2, nb, D_FF)


def _layer_spec(arr, layer):
    tail = (0,) * (arr.ndim - 1)
    return pl.BlockSpec((None,) + arr.shape[1:], lambda b, t: (layer,) + tail,
                        pipeline_mode=pl.Buffered(1))


def _x_spec(tt, nbt):
    return pl.BlockSpec((tt, nbt, D_MODEL), lambda b, t: (t, b, 0))


def _state_spec(steps, nbt, width):
    return pl.BlockSpec((steps, nbt, width), lambda b, t: (0, b, 0))


_PARAMS = pltpu.CompilerParams(dimension_semantics=("arbitrary", "arbitrary"),
                               vmem_limit_bytes=VMEM_LIMIT_BYTES)

_EVEN_KEYS = ("w_in", "bcat", "lam", "ccat", "dskip", "w_glu", "b_glu", "w_sconv", "w_out")
_ODD_KEYS = ("w1", "b1", "w_dw", "ln_g", "ln_b", "w2", "b2")
_FFN_KEYS = ("g_ffn", "w_up", "w_conv", "w_down")


def _even_call(x, h0, s0, w, layer, *, tt, nbt, meta=None):
    n_meta = 0 if meta is None else meta.shape[0]
    length, nb = x.shape[0] + n_meta, x.shape[1]
    grid = (nb // nbt, length // tt)
    w_specs = [_layer_spec(w["g_mix"], layer)] + [_layer_spec(w[k], layer // 2) for k in _EVEN_KEYS]
    if n_meta:
        x_specs = [pl.BlockSpec((pl.Element(tt), pl.Element(nbt), pl.Element(D_MODEL)),
                                lambda b, t: (jnp.maximum(t * tt - n_meta, 0), b * nbt, 0)),
                   pl.BlockSpec(meta.shape, lambda b, t: (0, 0))]
        x_args = (x, meta)
    else:
        x_specs, x_args = [_x_spec(tt, nbt)], (x,)
    return pl.pallas_call(
        functools.partial(_even_kernel, tt=tt, nb=nbt, n_meta=n_meta),
        grid=grid,
        in_specs=x_specs + [pl.BlockSpec((nbt, D_STATE), lambda b, t: (b, 0)),
                            _state_spec(SCONV_W - 1, nbt, D_SCONV)] + w_specs,
        out_specs=[_x_spec(tt, nbt),
                   pl.BlockSpec((nbt, D_STATE), lambda b, t: (b, 0)),
                   _state_spec(SCONV_W - 1, nbt, D_SCONV)],
        out_shape=[jax.ShapeDtypeStruct((length, nb, D_MODEL), _F32),
                   jax.ShapeDtypeStruct(h0.shape, _F32),
                   jax.ShapeDtypeStruct(s0.shape, _F32)],
        scratch_shapes=[pltpu.VMEM((nbt, D_STATE), _F32),
                        pltpu.VMEM(((SCONV_W - 1) * nbt, D_SCONV), _F32),
                        pltpu.VMEM((tt * nbt, 2 * KB_STATES), _F32)],
        compiler_params=_PARAMS,
        name="even_mixer",
    )(*x_args, h0, s0, w["g_mix"], *[w[k] for k in _EVEN_KEYS])


def _odd_call(x, c0, w, layer, *, tt, nbt):
    length, nb, _ = x.shape
    n_t = length // tt
    grid = (nb // nbt, n_t)
    w_specs = [_layer_spec(w["g_mix"], layer)] + [_layer_spec(w[k], layer // 2) for k in _ODD_KEYS]
    return pl.pallas_call(
        functools.partial(_odd_kernel, tt=tt, nb=nbt, n_t=n_t),
        grid=grid,
        in_specs=[_x_spec(tt, nbt), _state_spec(CONF_W - 1, nbt, D_CONF)] + w_specs,
        out_specs=[_x_spec(tt, nbt), _state_spec(CONF_W - 1, nbt, D_CONF)],
        out_shape=[jax.ShapeDtypeStruct(x.shape, _F32), jax.ShapeDtypeStruct(c0.shape, _F32)],
        scratch_shapes=[pltpu.VMEM(((CONF_W - 1 + tt) * nbt, D_CONF), _F32),
                        pltpu.VMEM((tt * nbt, D_CONF), _F32)],
        compiler_params=_PARAMS,
        name="odd_mixer",
    )(x, c0, w["g_mix"], *[w[k] for k in _ODD_KEYS])


def _ffn_call(x, f0, w, layer, *, tt, nbt, final_norm):
    length, nb, _ = x.shape
    grid = (nb // nbt, length // tt)
    w_specs = [_layer_spec(w[k], layer) for k in _FFN_KEYS] + [_layer_spec(w["g_final"], 0)]
    return pl.pallas_call(
        functools.partial(_ffn_kernel, tt=tt, nb=nbt, final_norm=final_norm),
        grid=grid,
        in_specs=[_x_spec(tt, nbt), _state_spec(FFN_W - 1, nbt, D_FF)] + w_specs,
        out_specs=[_x_spec(tt, nbt), _state_spec(FFN_W - 1, nbt, D_FF)],
        out_shape=[jax.ShapeDtypeStruct(x.shape, _F32), jax.ShapeDtypeStruct(f0.shape, _F32)],
        scratch_shapes=[pltpu.VMEM(((FFN_W - 1) * nbt, D_FF), _F32),
                        pltpu.VMEM((tt * nbt, D_FF), _BF16)],
        compiler_params=_PARAMS,
        name="conv_ffn",
    )(x, f0, *[w[k] for k in _FFN_KEYS], w["g_final"])


def _rows(v):
    return v.reshape(v.shape[0], 1, v.shape[-1]).astype(_F32)


def _prep_params(p):
    n_even = p["w_in_even"].shape[0]
    p_, s_ = SSM_STATE, SSM_GROUP
    lam = lax.complex(p["ssm_lam_re"].astype(_F32), p["ssm_lam_im"].astype(_F32))
    dt = jnp.exp(p["ssm_log_dt"].astype(_F32))[..., None]
    lam_bar = jnp.exp(lam * dt)
    b_bar = ((lam_bar - 1.0) / lam)[..., None] * lax.complex(
        p["ssm_b_re"].astype(_F32), p["ssm_b_im"].astype(_F32))
    eye = jnp.eye(KB_GROUPS, dtype=_F32)

    def b_block(v):
        v = v.reshape(n_even, N_KB, KB_GROUPS, p_, s_)
        return jnp.einsum("ekgpi,gh->ekgihp", v, eye).reshape(n_even, N_KB, KB_COLS, KB_STATES)

    def c_block(v):
        v = v.reshape(n_even, N_KB, KB_GROUPS, s_, p_)
        return jnp.einsum("ekgip,gh->ekgphi", v, eye).reshape(n_even, N_KB, KB_STATES, KB_COLS)

    bcat = jnp.concatenate([b_block(jnp.real(b_bar)), b_block(jnp.imag(b_bar))], axis=3)
    ccat = jnp.concatenate([c_block(p["ssm_c_re"].astype(_F32)),
                            -c_block(p["ssm_c_im"].astype(_F32))], axis=2)
    lamcat = jnp.stack([jnp.real(lam_bar).reshape(n_even, N_KB, KB_STATES),
                        jnp.imag(lam_bar).reshape(n_even, N_KB, KB_STATES)], axis=2)
    return dict(
        g_mix=_rows(p["norm_mix"]), g_ffn=_rows(p["norm_ffn"]), g_final=_rows(p["norm_final"][None]),
        w_in=p["w_in_even"].astype(_BF16), bcat=bcat.astype(_BF16), lam=lamcat,
        ccat=ccat.astype(_BF16), dskip=_rows(p["ssm_d"]), w_glu=p["w_glu"].astype(_BF16),
        b_glu=_rows(p["b_glu"]), w_sconv=p["w_sconv"].astype(_F32),
        w_out=p["w_out_even"].astype(_BF16),
        w1=p["w_conf_pw1"].astype(_BF16), b1=_rows(p["b_conf_pw1"]),
        w_dw=jnp.repeat(p["w_conf_dw"].astype(_F32), SUBLANES, axis=1),
        ln_g=_rows(p["conf_ln_g"]), ln_b=_rows(p["conf_ln_b"]),
        w2=p["w_conf_pw2"].astype(_BF16), b2=_rows(p["b_conf_pw2"]),
        w_up=p["w_ffn_up"].astype(_BF16), w_conv=p["w_ffn_conv"].astype(_F32),
        w_down=p["w_ffn_down"].astype(_BF16))


def _time_major(v):
    return jnp.transpose(v, (1, 0, 2))


def _ssm_to_flat(re, im):
    nb = re.shape[0]
    re = re.reshape(nb, N_KB, KB_STATES)
    im = im.reshape(nb, N_KB, KB_STATES)
    return jnp.concatenate([re, im], axis=2).reshape(nb, D_STATE).astype(_F32)


def _flat_to_ssm(h):
    nb = h.shape[0]
    h = h.reshape(nb, N_KB, 2, KB_STATES)
    shape = (nb, N_SSM_GROUPS, SSM_STATE)
    return h[:, :, 0].reshape(shape), h[:, :, 1].reshape(shape)


def _trunk(x, ssm_re, ssm_im, sconv, cconv, ffn_buf, w, *, tt, tt_mix, nbt, nbt_odd, meta=None):
    depth = w["g_mix"].shape[0]
    n_re, n_im, n_s, n_c, n_f = [], [], [], [], []
    for l in range(depth):
        i = l // 2
        if l % 2 == 0:
            h0 = _ssm_to_flat(ssm_re[i], ssm_im[i])
            x, h1, s1 = _even_call(x, h0, _time_major(sconv[i]), w, l, tt=tt_mix, nbt=nbt,
                                   meta=meta if l == 0 else None)
            re, im = _flat_to_ssm(h1)
            n_re.append(re)
            n_im.append(im)
            n_s.append(_time_major(s1))
        else:
            x, c1 = _odd_call(x, _time_major(cconv[i]), w, l, tt=tt_mix, nbt=nbt_odd)
            n_c.append(_time_major(c1))
        x, f1 = _ffn_call(x, _time_major(ffn_buf[l]), w, l, tt=tt, nbt=nbt, final_norm=(l == depth - 1))
        n_f.append(_time_major(f1))
    return x, jnp.stack(n_re), jnp.stack(n_im), jnp.stack(n_s), jnp.stack(n_c), jnp.stack(n_f)


def kernel(x_prompt, x_sample, state_ssm_re, state_ssm_im, state_sconv, state_cconv, state_ffn, meta_tokens, norm_mix, norm_ffn, norm_final, w_in_even, ssm_lam_re, ssm_lam_im, ssm_log_dt, ssm_b_re, ssm_b_im, ssm_c_re, ssm_c_im, ssm_d, w_glu, b_glu, w_sconv, w_out_even, w_conf_pw1, b_conf_pw1, w_conf_dw, conf_ln_g, conf_ln_b, w_conf_pw2, b_conf_pw2, w_ffn_up, w_ffn_conv, w_ffn_down):
    w = _prep_params(dict(
        norm_mix=norm_mix, norm_ffn=norm_ffn, norm_final=norm_final, w_in_even=w_in_even,
        ssm_lam_re=ssm_lam_re, ssm_lam_im=ssm_lam_im, ssm_log_dt=ssm_log_dt, ssm_b_re=ssm_b_re,
        ssm_b_im=ssm_b_im, ssm_c_re=ssm_c_re, ssm_c_im=ssm_c_im, ssm_d=ssm_d, w_glu=w_glu,
        b_glu=b_glu, w_sconv=w_sconv, w_out_even=w_out_even, w_conf_pw1=w_conf_pw1,
        b_conf_pw1=b_conf_pw1, w_conf_dw=w_conf_dw, conf_ln_g=conf_ln_g, conf_ln_b=conf_ln_b,
        w_conf_pw2=w_conf_pw2, b_conf_pw2=b_conf_pw2, w_ffn_up=w_ffn_up, w_ffn_conv=w_ffn_conv,
        w_ffn_down=w_ffn_down))
    depth = norm_mix.shape[0]
    batch, seq, _ = x_prompt.shape
    dec_batch, dec_seq, _ = x_sample.shape
    dt = x_prompt.dtype
    n_even, n_odd = (depth + 1) // 2, depth // 2

    xp = jnp.transpose(x_prompt, (1, 0, 2))
    zp_h = jnp.zeros((n_even, batch, N_SSM_GROUPS, SSM_STATE), dt)
    zp_s = jnp.zeros((n_even, batch, SCONV_W - 1, D_SCONV), dt)
    zp_c = jnp.zeros((n_odd, batch, CONF_W - 1, D_CONF), dt)
    zp_f = jnp.zeros((depth, batch, FFN_W - 1, D_FF), dt)
    yp, *p_states = _trunk(xp, zp_h, zp_h, zp_s, zp_c, zp_f, w,
                           tt=PROMPT_TILE_STEPS, tt_mix=PROMPT_MIXER_TILE_STEPS, nbt=batch, nbt_odd=batch,
                           meta=meta_tokens.astype(dt))
    y_prompt = jnp.transpose(yp[N_META:], (1, 0, 2))

    xs = jnp.transpose(x_sample, (1, 0, 2))
    ys, *s_states = _trunk(xs, state_ssm_re, state_ssm_im, state_sconv, state_cconv, state_ffn, w,
                           tt=dec_seq, tt_mix=dec_seq, nbt=dec_batch, nbt_odd=SAMPLE_ODD_BATCH_TILE)
    y_sample = jnp.transpose(ys, (1, 0, 2))

    return (y_prompt, y_sample, *p_states, *s_states)
```

```python
import functools
import math

import jax
import jax.numpy as jnp
from jax import lax
from jax.experimental import pallas as pl
from jax.experimental.pallas import tpu as pltpu

D_MODEL = 1024
N_META = 16
SSM_GROUP = 16
SSM_STATE = 64
D_SSM = D_MODEL // 2
N_SSM_GROUPS = D_SSM // SSM_GROUP
D_SCONV = D_MODEL - D_SSM
SCONV_W = 3
D_IN_EVEN = D_SSM + 3 * D_SCONV
D_CONF = D_MODEL
CONF_W = 31
D_FF = ((8 * D_MODEL // 3 + 127) // 128) * 128
FFN_W = 3
EPS = 1e-6

LANES = 128
SUBLANES = 8
KB_COLS = LANES
N_KB = D_SSM // KB_COLS
KB_GROUPS = KB_COLS // SSM_GROUP
KB_STATES = KB_GROUPS * SSM_STATE
D_STATE = N_KB * 2 * KB_STATES
FFN_CHUNK = 256
PW1_SLICE = 256
VMEM_LIMIT_BYTES = 56 * 1024 * 1024
PROMPT_TILE_STEPS = 86
PROMPT_MIXER_TILE_STEPS = 129
SAMPLE_ODD_BATCH_TILE = 32
CONV_BLOCK_TILES = 8

_F32 = jnp.float32
_BF16 = jnp.bfloat16


def _dot(a, b):
    return jnp.dot(a, b, preferred_element_type=_F32)


def _rms(x, g):
    return (x * lax.rsqrt(jnp.mean(x * x, axis=-1, keepdims=True) + EPS)) * g


def _sigmoid(x):
    return 1.0 / (1.0 + jnp.exp(-x))


def _gelu_tanh(x):
    c = math.sqrt(2.0 / math.pi)
    return x * (0.5 * (1.0 + jnp.tanh(c * (x + 0.044715 * (x * x * x)))))


def _shifted(carry, cur, nb, k, width):
    back = (width - 1 - k) * nb
    if back == 0:
        return cur
    tm = cur.shape[0]
    head = carry[carry.shape[0] - back:]
    if back >= tm:
        return head[:tm]
    return jnp.concatenate([head, cur[:tm - back]], axis=0)


def _short_conv(carry, cur, w, nb):
    width = w.shape[0]
    acc = None
    for k in range(width):
        term = w[k:k + 1, :] * _shifted(carry, cur, nb, k, width)
        acc = term if acc is None else acc + term
    tail = jnp.concatenate([carry, cur], axis=0)[cur.shape[0]:]
    return acc, tail


def _even_kernel(*refs, tt, nb, n_meta):
    if n_meta:
        x_ref, meta_ref, *refs = refs
    else:
        x_ref, *refs = refs
    (h0_ref, s0_ref, g_ref, win_ref, bcat_ref, lam_ref, ccat_ref, dskip_ref, wglu_ref, bglu_ref,
     wsc_ref, wout_ref, xo_ref, ho_ref, so_ref, hcar_ref, scar_ref, hb_ref) = refs
    t_idx = pl.program_id(1)
    n_t = pl.num_programs(1)
    tm = tt * nb

    @pl.when(t_idx == 0)
    def _():
        hcar_ref[...] = h0_ref[...]
        scar_ref[...] = s0_ref[...].reshape(2 * nb, D_SCONV)

    x = x_ref[...].reshape(tm, D_MODEL)
    if n_meta:
        meta = jnp.broadcast_to(meta_ref[...][:, None, :], (n_meta, nb, D_MODEL))
        first = jnp.concatenate([meta.reshape(n_meta * nb, D_MODEL), x[:tm - n_meta * nb]], axis=0)
        x = jnp.where(t_idx == 0, first, x)
    xn = _rms(x, g_ref[...]).astype(_BF16)
    u = _dot(xn, win_ref[:, :D_SSM])
    ub = u.astype(_BF16)

    y_parts = []
    for kb in range(N_KB):
        c0 = kb * 2 * KB_STATES
        hb_ref[...] = _dot(ub[:, kb * KB_COLS:(kb + 1) * KB_COLS], bcat_ref[kb])
        lam_re = jnp.broadcast_to(lam_ref[kb, 0:1, :], (SUBLANES, KB_STATES))
        lam_im = jnp.broadcast_to(lam_ref[kb, 1:2, :], (SUBLANES, KB_STATES))
        for j in range(nb // SUBLANES):
            rows = slice(j * SUBLANES, (j + 1) * SUBLANES)
            h_re0 = hcar_ref[rows, c0:c0 + KB_STATES]
            h_im0 = hcar_ref[rows, c0 + KB_STATES:c0 + 2 * KB_STATES]

            def step(t, carry, j=j, lam_re=lam_re, lam_im=lam_im):
                h_re, h_im = carry
                r = pl.ds(t * nb + j * SUBLANES, SUBLANES)
                n_re = lam_re * h_re - lam_im * h_im + hb_ref[r, 0:KB_STATES]
                n_im = lam_re * h_im + lam_im * h_re + hb_ref[r, KB_STATES:2 * KB_STATES]
                hb_ref[r, 0:KB_STATES] = n_re
                hb_ref[r, KB_STATES:2 * KB_STATES] = n_im
                return n_re, n_im

            h_re, h_im = h_re0, h_im0
            for t in range(tt):
                h_re, h_im = step(t, (h_re, h_im))
            hcar_ref[rows, c0:c0 + KB_STATES] = h_re
            hcar_ref[rows, c0 + KB_STATES:c0 + 2 * KB_STATES] = h_im
        y_parts.append(_dot(hb_ref[...].astype(_BF16), ccat_ref[kb]))
    y = jnp.concatenate(y_parts, axis=1) + dskip_ref[...] * u

    y_a = _gelu_tanh(y)
    y_a = y_a * _sigmoid(_dot(y_a.astype(_BF16), wglu_ref[...]) + bglu_ref[...])

    x_b = _dot(xn, win_ref[:, D_SSM:D_SSM + D_SCONV])
    g_c = _dot(xn, win_ref[:, D_SSM + 2 * D_SCONV:])
    g_b = _dot(xn, win_ref[:, D_SSM + D_SCONV:D_SSM + 2 * D_SCONV])
    conv, tail = _short_conv(scar_ref[...], g_c * x_b, wsc_ref[...], nb)
    scar_ref[...] = tail
    y_b = g_b * conv

    mixed = jnp.concatenate([y_a.astype(_BF16), y_b.astype(_BF16)], axis=1)
    xo_ref[...] = (x + _dot(mixed, wout_ref[...])).reshape(tt, nb, D_MODEL)

    @pl.when(t_idx == n_t - 1)
    def _():
        ho_ref[...] = hcar_ref[...]
        so_ref[...] = scar_ref[...].reshape(2, nb, D_SCONV)


def _odd_kernel(x_ref, c0_ref, g_ref, w1_ref, b1_ref, wdw_ref, lng_ref, lnb_ref, w2_ref, b2_ref,
                xo_ref, co_ref,
                win_ref, conv_ref, *, tt, nb, n_t):
    t_idx = pl.program_id(1)
    tm = tt * nb
    halo = (CONF_W - 1) * nb

    @pl.when(t_idx == 0)
    def _():
        win_ref[0:halo, :] = c0_ref[...].reshape(halo, D_CONF)

    x = x_ref[...].reshape(tm, D_MODEL)
    xn = _rms(x, g_ref[...]).astype(_BF16)
    for lo in range(0, D_CONF, PW1_SLICE):
        cols = slice(lo, lo + PW1_SLICE)
        gcols = slice(D_CONF + lo, D_CONF + lo + PW1_SLICE)
        a = _dot(xn, w1_ref[:, cols]) + b1_ref[:, cols]
        gate = _dot(xn, w1_ref[:, gcols]) + b1_ref[:, gcols]
        win_ref[halo:halo + tm, cols] = a * _sigmoid(gate)

    def conv_block(r0, n_tiles):
        view = win_ref.at[pl.ds(r0, halo + n_tiles * SUBLANES), :]
        for c in range(D_CONF // LANES):
            lanes = slice(c * LANES, (c + 1) * LANES)
            accs = [None] * n_tiles
            for k in range(CONF_W):
                tap = wdw_ref[k * SUBLANES:(k + 1) * SUBLANES, lanes]
                for j in range(n_tiles):
                    lo = k * nb + j * SUBLANES
                    term = tap * view[lo:lo + SUBLANES, lanes]
                    accs[j] = term if accs[j] is None else accs[j] + term
            for j in range(n_tiles):
                conv_ref[pl.ds(r0 + j * SUBLANES, SUBLANES), lanes] = accs[j]

    n_row_tiles = tm // SUBLANES
    block_rows = CONV_BLOCK_TILES * SUBLANES

    def conv_trip(i, _):
        conv_block(pl.multiple_of(i * block_rows, block_rows), CONV_BLOCK_TILES)
        return 0

    lax.fori_loop(0, n_row_tiles // CONV_BLOCK_TILES, conv_trip, 0)
    if n_row_tiles % CONV_BLOCK_TILES:
        conv_block(n_row_tiles // CONV_BLOCK_TILES * block_rows, n_row_tiles % CONV_BLOCK_TILES)

    conv = conv_ref[...]
    mu = jnp.mean(conv, axis=-1, keepdims=True)
    xc = conv - mu
    yv = xc * lax.rsqrt(jnp.mean(xc * xc, axis=-1, keepdims=True) + EPS)
    yv = yv * lng_ref[...] + lnb_ref[...]
    act = (yv * _sigmoid(yv)).astype(_BF16)

    out = _dot(act, w2_ref[...]) + b2_ref[...]
    xo_ref[...] = (x + out).reshape(tt, nb, D_MODEL)

    @pl.when(t_idx == n_t - 1)
    def _():
        co_ref[...] = win_ref[tm:tm + halo, :].reshape(CONF_W - 1, nb, D_CONF)

    if n_t > 1:
        assert tt >= CONF_W - 1

        @pl.when(t_idx < n_t - 1)
        def _():
            win_ref[0:halo, :] = win_ref[tm:tm + halo, :]


def _ffn_kernel(x_ref, f0_ref, g_ref, wup_ref, wc_ref, wd_ref, gfin_ref,
                xo_ref, fo_ref,
                fcar_ref, hid_ref, *, tt, nb, final_norm):
    t_idx = pl.program_id(1)
    n_t = pl.num_programs(1)
    tm = tt * nb

    @pl.when(t_idx == 0)
    def _():
        fcar_ref[...] = f0_ref[...].reshape(2 * nb, D_FF)

    x = x_ref[...].reshape(tm, D_MODEL)
    xn = _rms(x, g_ref[...]).astype(_BF16)
    for c in range(D_FF // FFN_CHUNK):
        cols = slice(c * FFN_CHUNK, (c + 1) * FFN_CHUNK)
        a = _dot(xn, wup_ref[:, cols])
        gate = _dot(xn, wup_ref[:, D_FF + c * FFN_CHUNK:D_FF + (c + 1) * FFN_CHUNK])
        a_c, tail = _short_conv(fcar_ref[:, cols], a, wc_ref[:, cols], nb)
        fcar_ref[:, cols] = tail
        hid_ref[:, cols] = ((a_c * _sigmoid(a_c)) * gate).astype(_BF16)
    y = x + _dot(hid_ref[...], wd_ref[...])
    if final_norm:
        y = _rms(y, gfin_ref[...])
    xo_ref[...] = y.reshape(tt, nb, D_MODEL)

    @pl.when(t_idx == n_t - 1)
    def _():
        fo_ref[...] = fcar_ref[...].reshape(2, nb, D_FF)


def _layer_spec(arr, layer):
    tail = (0,) * (arr.ndim - 1)
    return pl.BlockSpec((None,) + arr.shape[1:], lambda b, t: (layer,) + tail,
                        pipeline_mode=pl.Buffered(1))


def _x_spec(tt, nbt):
    return pl.BlockSpec((tt, nbt, D_MODEL), lambda b, t: (t, b, 0))


def _state_spec(steps, nbt, width):
    return pl.BlockSpec((steps, nbt, width), lambda b, t: (0, b, 0))


_PARAMS = pltpu.CompilerParams(dimension_semantics=("arbitrary", "arbitrary"),
                               vmem_limit_bytes=VMEM_LIMIT_BYTES)

_EVEN_KEYS = ("w_in", "bcat", "lam", "ccat", "dskip", "w_glu", "b_glu", "w_sconv", "w_out")
_ODD_KEYS = ("w1", "b1", "w_dw", "ln_g", "ln_b", "w2", "b2")
_FFN_KEYS = ("g_ffn", "w_up", "w_conv", "w_down")


def _even_call(x, h0, s0, w, layer, *, tt, nbt, meta=None):
    n_meta = 0 if meta is None else meta.shape[0]
    length, nb = x.shape[0] + n_meta, x.shape[1]
    grid = (nb // nbt, length // tt)
    w_specs = [_layer_spec(w["g_mix"], layer)] + [_layer_spec(w[k], layer // 2) for k in _EVEN_KEYS]
    if n_meta:
        x_specs = [pl.BlockSpec((pl.Element(tt), pl.Element(nbt), pl.Element(D_MODEL)),
                                lambda b, t: (jnp.maximum(t * tt - n_meta, 0), b * nbt, 0)),
                   pl.BlockSpec(meta.shape, lambda b, t: (0, 0))]
        x_args = (x, meta)
    else:
        x_specs, x_args = [_x_spec(tt, nbt)], (x,)
    return pl.pallas_call(
        functools.partial(_even_kernel, tt=tt, nb=nbt, n_meta=n_meta),
        grid=grid,
        in_specs=x_specs + [pl.BlockSpec((nbt, D_STATE), lambda b, t: (b, 0)),
                            _state_spec(SCONV_W - 1, nbt, D_SCONV)] + w_specs,
        out_specs=[_x_spec(tt, nbt),
                   pl.BlockSpec((nbt, D_STATE), lambda b, t: (b, 0)),
                   _state_spec(SCONV_W - 1, nbt, D_SCONV)],
        out_shape=[jax.ShapeDtypeStruct((length, nb, D_MODEL), _F32),
                   jax.ShapeDtypeStruct(h0.shape, _F32),
                   jax.ShapeDtypeStruct(s0.shape, _F32)],
        scratch_shapes=[pltpu.VMEM((nbt, D_STATE), _F32),
                        pltpu.VMEM(((SCONV_W - 1) * nbt, D_SCONV), _F32),
                        pltpu.VMEM((tt * nbt, 2 * KB_STATES), _F32)],
        compiler_params=_PARAMS,
        name="even_mixer",
    )(*x_args, h0, s0, w["g_mix"], *[w[k] for k in _EVEN_KEYS])


def _odd_call(x, c0, w, layer, *, tt, nbt):
    length, nb, _ = x.shape
    n_t = length // tt
    grid = (nb // nbt, n_t)
    w_specs = [_layer_spec(w["g_mix"], layer)] + [_layer_spec(w[k], layer // 2) for k in _ODD_KEYS]
    return pl.pallas_call(
        functools.partial(_odd_kernel, tt=tt, nb=nbt, n_t=n_t),
        grid=grid,
        in_specs=[_x_spec(tt, nbt), _state_spec(CONF_W - 1, nbt, D_CONF)] + w_specs,
        out_specs=[_x_spec(tt, nbt), _state_spec(CONF_W - 1, nbt, D_CONF)],
        out_shape=[jax.ShapeDtypeStruct(x.shape, _F32), jax.ShapeDtypeStruct(c0.shape, _F32)],
        scratch_shapes=[pltpu.VMEM(((CONF_W - 1 + tt) * nbt, D_CONF), _F32),
                        pltpu.VMEM((tt * nbt, D_CONF), _F32)],
        compiler_params=_PARAMS,
        name="odd_mixer",
    )(x, c0, w["g_mix"], *[w[k] for k in _ODD_KEYS])


def _ffn_call(x, f0, w, layer, *, tt, nbt, final_norm):
    length, nb, _ = x.shape
    grid = (nb // nbt, length // tt)
    w_specs = [_layer_spec(w[k], layer) for k in _FFN_KEYS] + [_layer_spec(w["g_final"], 0)]
    return pl.pallas_call(
        functools.partial(_ffn_kernel, tt=tt, nb=nbt, final_norm=final_norm),
        grid=grid,
        in_specs=[_x_spec(tt, nbt), _state_spec(FFN_W - 1, nbt, D_FF)] + w_specs,
        out_specs=[_x_spec(tt, nbt), _state_spec(FFN_W - 1, nbt, D_FF)],
        out_shape=[jax.ShapeDtypeStruct(x.shape, _F32), jax.ShapeDtypeStruct(f0.shape, _F32)],
        scratch_shapes=[pltpu.VMEM(((FFN_W - 1) * nbt, D_FF), _F32),
                        pltpu.VMEM((tt * nbt, D_FF), _BF16)],
        compiler_params=_PARAMS,
        name="conv_ffn",
    )(x, f0, *[w[k] for k in _FFN_KEYS], w["g_final"])


def _rows(v):
    return v.reshape(v.shape[0], 1, v.shape[-1]).astype(_F32)


def _prep_params(p):
    n_even = p["w_in_even"].shape[0]
    p_, s_ = SSM_STATE, SSM_GROUP
    lam = lax.complex(p["ssm_lam_re"].astype(_F32), p["ssm_lam_im"].astype(_F32))
    dt = jnp.exp(p["ssm_log_dt"].astype(_F32))[..., None]
    lam_bar = jnp.exp(lam * dt)
    b_bar = ((lam_bar - 1.0) / lam)[..., None] * lax.complex(
        p["ssm_b_re"].astype(_F32), p["ssm_b_im"].astype(_F32))
    eye = jnp.eye(KB_GROUPS, dtype=_F32)

    def b_block(v):
        v = v.reshape(n_even, N_KB, KB_GROUPS, p_, s_)
        return jnp.einsum("ekgpi,gh->ekgihp", v, eye).reshape(n_even, N_KB, KB_COLS, KB_STATES)

    def c_block(v):
        v = v.reshape(n_even, N_KB, KB_GROUPS, s_, p_)
        return jnp.einsum("ekgip,gh->ekgphi", v, eye).reshape(n_even, N_KB, KB_STATES, KB_COLS)

    bcat = jnp.concatenate([b_block(jnp.real(b_bar)), b_block(jnp.imag(b_bar))], axis=3)
    ccat = jnp.concatenate([c_block(p["ssm_c_re"].astype(_F32)),
                            -c_block(p["ssm_c_im"].astype(_F32))], axis=2)
    lamcat = jnp.stack([jnp.real(lam_bar).reshape(n_even, N_KB, KB_STATES),
                        jnp.imag(lam_bar).reshape(n_even, N_KB, KB_STATES)], axis=2)
    return dict(
        g_mix=_rows(p["norm_mix"]), g_ffn=_rows(p["norm_ffn"]), g_final=_rows(p["norm_final"][None]),
        w_in=p["w_in_even"].astype(_BF16), bcat=bcat.astype(_BF16), lam=lamcat,
        ccat=ccat.astype(_BF16), dskip=_rows(p["ssm_d"]), w_glu=p["w_glu"].astype(_BF16),
        b_glu=_rows(p["b_glu"]), w_sconv=p["w_sconv"].astype(_F32),
        w_out=p["w_out_even"].astype(_BF16),
        w1=p["w_conf_pw1"].astype(_BF16), b1=_rows(p["b_conf_pw1"]),
        w_dw=jnp.repeat(p["w_conf_dw"].astype(_F32), SUBLANES, axis=1),
        ln_g=_rows(p["conf_ln_g"]), ln_b=_rows(p["conf_ln_b"]),
        w2=p["w_conf_pw2"].astype(_BF16), b2=_rows(p["b_conf_pw2"]),
        w_up=p["w_ffn_up"].astype(_BF16), w_conv=p["w_ffn_conv"].astype(_F32),
        w_down=p["w_ffn_down"].astype(_BF16))


def _time_major(v):
    return jnp.transpose(v, (1, 0, 2))


def _ssm_to_flat(re, im):
    nb = re.shape[0]
    re = re.reshape(nb, N_KB, KB_STATES)
    im = im.reshape(nb, N_KB, KB_STATES)
    return jnp.concatenate([re, im], axis=2).reshape(nb, D_STATE).astype(_F32)


def _flat_to_ssm(h):
    nb = h.shape[0]
    h = h.reshape(nb, N_KB, 2, KB_STATES)
    shape = (nb, N_SSM_GROUPS, SSM_STATE)
    return h[:, :, 0].reshape(shape), h[:, :, 1].reshape(shape)


def _trunk(x, ssm_re, ssm_im, sconv, cconv, ffn_buf, w, *, tt, tt_mix, nbt, nbt_odd, meta=None):
    depth = w["g_mix"].shape[0]
    n_re, n_im, n_s, n_c, n_f = [], [], [], [], []
    for l in range(depth):
        i = l // 2
        if l % 2 == 0:
            h0 = _ssm_to_flat(ssm_re[i], ssm_im[i])
            x, h1, s1 = _even_call(x, h0, _time_major(sconv[i]), w, l, tt=tt_mix, nbt=nbt,
                                   meta=meta if l == 0 else None)
            re, im = _flat_to_ssm(h1)
            n_re.append(re)
            n_im.append(im)
            n_s.append(_time_major(s1))
        else:
            x, c1 = _odd_call(x, _time_major(cconv[i]), w, l, tt=tt_mix, nbt=nbt_odd)
            n_c.append(_time_major(c1))
        x, f1 = _ffn_call(x, _time_major(ffn_buf[l]), w, l, tt=tt, nbt=nbt, final_norm=(l == depth - 1))
        n_f.append(_time_major(f1))
    return x, jnp.stack(n_re), jnp.stack(n_im), jnp.stack(n_s), jnp.stack(n_c), jnp.stack(n_f)


def kernel(x_prompt, x_sample, state_ssm_re, state_ssm_im, state_sconv, state_cconv, state_ffn, meta_tokens, norm_mix, norm_ffn, norm_final, w_in_even, ssm_lam_re, ssm_lam_im, ssm_log_dt, ssm_b_re, ssm_b_im, ssm_c_re, ssm_c_im, ssm_d, w_glu, b_glu, w_sconv, w_out_even, w_conf_pw1, b_conf_pw1, w_conf_dw, conf_ln_g, conf_ln_b, w_conf_pw2, b_conf_pw2, w_ffn_up, w_ffn_conv, w_ffn_down):
    w = _prep_params(dict(
        norm_mix=norm_mix, norm_ffn=norm_ffn, norm_final=norm_final, w_in_even=w_in_even,
        ssm_lam_re=ssm_lam_re, ssm_lam_im=ssm_lam_im, ssm_log_dt=ssm_log_dt, ssm_b_re=ssm_b_re,
        ssm_b_im=ssm_b_im, ssm_c_re=ssm_c_re, ssm_c_im=ssm_c_im, ssm_d=ssm_d, w_glu=w_glu,
        b_glu=b_glu, w_sconv=w_sconv, w_out_even=w_out_even, w_conf_pw1=w_conf_pw1,
        b_conf_pw1=b_conf_pw1, w_conf_dw=w_conf_dw, conf_ln_g=conf_ln_g, conf_ln_b=conf_ln_b,
        w_conf_pw2=w_conf_pw2, b_conf_pw2=b_conf_pw2, w_ffn_up=w_ffn_up, w_ffn_conv=w_ffn_conv,
        w_ffn_down=w_ffn_down))
    depth = norm_mix.shape[0]
    batch, seq, _ = x_prompt.shape
    dec_batch, dec_seq, _ = x_sample.shape
    dt = x_prompt.dtype
    n_even, n_odd = (depth + 1) // 2, depth // 2

    xs = jnp.transpose(x_sample, (1, 0, 2))
    ys, *s_states = _trunk(xs, state_ssm_re, state_ssm_im, state_sconv, state_cconv, state_ffn, w,
                           tt=dec_seq, tt_mix=dec_seq, nbt=dec_batch, nbt_odd=SAMPLE_ODD_BATCH_TILE)
    y_sample = jnp.transpose(ys, (1, 0, 2))

    xp = jnp.transpose(x_prompt, (1, 0, 2))
    zp_h = jnp.zeros((n_even, batch, N_SSM_GROUPS, SSM_STATE), dt)
    zp_s = jnp.zeros((n_even, batch, SCONV_W - 1, D_SCONV), dt)
    zp_c = jnp.zeros((n_odd, batch, CONF_W - 1, D_CONF), dt)
    zp_f = jnp.zeros((depth, batch, FFN_W - 1, D_FF), dt)
    yp, *p_states = _trunk(xp, zp_h, zp_h, zp_s, zp_c, zp_f, w,
                           tt=PROMPT_TILE_STEPS, tt_mix=PROMPT_MIXER_TILE_STEPS, nbt=batch, nbt_odd=batch,
                           meta=meta_tokens.astype(dt))
    y_prompt = jnp.transpose(yp[N_META:], (1, 0, 2))

    return (y_prompt, y_sample, *p_states, *s_states)
```

```python
import functools
import math

import jax
import jax.numpy as jnp
from jax import lax
from jax.experimental import pallas as pl
from jax.experimental.pallas import tpu as pltpu

D_MODEL = 1024
N_META = 16
SSM_GROUP = 16
SSM_STATE = 64
D_SSM = D_MODEL // 2
N_SSM_GROUPS = D_SSM // SSM_GROUP
D_SCONV = D_MODEL - D_SSM
SCONV_W = 3
D_IN_EVEN = D_SSM + 3 * D_SCONV
D_CONF = D_MODEL
CONF_W = 31
D_FF = ((8 * D_MODEL // 3 + 127) // 128) * 128
FFN_W = 3
EPS = 1e-6

LANES = 128
SUBLANES = 8
KB_COLS = LANES
N_KB = D_SSM // KB_COLS
KB_GROUPS = KB_COLS // SSM_GROUP
KB_STATES = KB_GROUPS * SSM_STATE
D_STATE = N_KB * 2 * KB_STATES
FFN_CHUNK = 256
PW1_SLICE = 256
VMEM_LIMIT_BYTES = 56 * 1024 * 1024
PROMPT_TILE_STEPS = 86
PROMPT_MIXER_TILE_STEPS = 129
SAMPLE_ODD_BATCH_TILE = 32
CONV_BLOCK_TILES = 8

_F32 = jnp.float32
_BF16 = jnp.bfloat16


def _dot(a, b):
    return jnp.dot(a, b, preferred_element_type=_F32)


def _rms(x, g):
    return (x * lax.rsqrt(jnp.mean(x * x, axis=-1, keepdims=True) + EPS)) * g


def _sigmoid(x):
    return 1.0 / (1.0 + jnp.exp(-x))


def _gelu_tanh(x):
    c = math.sqrt(2.0 / math.pi)
    return x * (0.5 * (1.0 + jnp.tanh(c * (x + 0.044715 * (x * x * x)))))


def _shifted(carry, cur, nb, k, width):
    back = (width - 1 - k) * nb
    if back == 0:
        return cur
    tm = cur.shape[0]
    head = carry[carry.shape[0] - back:]
    if back >= tm:
        return head[:tm]
    return jnp.concatenate([head, cur[:tm - back]], axis=0)


def _short_conv(carry, cur, w, nb):
    width = w.shape[0]
    acc = None
    for k in range(width):
        term = w[k:k + 1, :] * _shifted(carry, cur, nb, k, width)
        acc = term if acc is None else acc + term
    tail = jnp.concatenate([carry, cur], axis=0)[cur.shape[0]:]
    return acc, tail


def _even_kernel(*refs, tt, nb, n_meta):
    if n_meta:
        x_ref, meta_ref, *refs = refs
    else:
        x_ref, *refs = refs
    (h0_ref, s0_ref, g_ref, win_ref, bcat_ref, lam_ref, ccat_ref, dskip_ref, wglu_ref, bglu_ref,
     wsc_ref, wout_ref, xo_ref, ho_ref, so_ref, hcar_ref, scar_ref, hb_ref) = refs
    t_idx = pl.program_id(1)
    n_t = pl.num_programs(1)
    tm = tt * nb

    @pl.when(t_idx == 0)
    def _():
        hcar_ref[...] = h0_ref[...]
        scar_ref[...] = s0_ref[...].reshape(2 * nb, D_SCONV)

    x = x_ref[...].reshape(tm, D_MODEL)
    if n_meta:
        meta = jnp.broadcast_to(meta_ref[...][:, None, :], (n_meta, nb, D_MODEL))
        first = jnp.concatenate([meta.reshape(n_meta * nb, D_MODEL), x[:tm - n_meta * nb]], axis=0)
        x = jnp.where(t_idx == 0, first, x)
    xn = _rms(x, g_ref[...]).astype(_BF16)
    u = _dot(xn, win_ref[:, :D_SSM])
    ub = u.astype(_BF16)

    y_parts = []
    for kb in range(N_KB):
        c0 = kb * 2 * KB_STATES
        hb_ref[...] = _dot(ub[:, kb * KB_COLS:(kb + 1) * KB_COLS], bcat_ref[kb])
        lam_re = jnp.broadcast_to(lam_ref[kb, 0:1, :], (SUBLANES, KB_STATES))
        lam_im = jnp.broadcast_to(lam_ref[kb, 1:2, :], (SUBLANES, KB_STATES))
        for j in range(nb // SUBLANES):
            rows = slice(j * SUBLANES, (j + 1) * SUBLANES)
            h_re0 = hcar_ref[rows, c0:c0 + KB_STATES]
            h_im0 = hcar_ref[rows, c0 + KB_STATES:c0 + 2 * KB_STATES]

            def step(t, carry, j=j, lam_re=lam_re, lam_im=lam_im):
                h_re, h_im = carry
                r = pl.ds(t * nb + j * SUBLANES, SUBLANES)
                n_re = lam_re * h_re - lam_im * h_im + hb_ref[r, 0:KB_STATES]
                n_im = lam_re * h_im + lam_im * h_re + hb_ref[r, KB_STATES:2 * KB_STATES]
                hb_ref[r, 0:KB_STATES] = n_re
                hb_ref[r, KB_STATES:2 * KB_STATES] = n_im
                return n_re, n_im

            h_re, h_im = h_re0, h_im0
            for t in range(tt):
                h_re, h_im = step(t, (h_re, h_im))
            hcar_ref[rows, c0:c0 + KB_STATES] = h_re
            hcar_ref[rows, c0 + KB_STATES:c0 + 2 * KB_STATES] = h_im
        y_parts.append(_dot(hb_ref[...].astype(_BF16), ccat_ref[kb]))
    y = jnp.concatenate(y_parts, axis=1) + dskip_ref[...] * u

    y_a = _gelu_tanh(y)
    y_a = y_a * _sigmoid(_dot(y_a.astype(_BF16), wglu_ref[...]) + bglu_ref[...])

    x_b = _dot(xn, win_ref[:, D_SSM:D_SSM + D_SCONV])
    g_c = _dot(xn, win_ref[:, D_SSM + 2 * D_SCONV:])
    g_b = _dot(xn, win_ref[:, D_SSM + D_SCONV:D_SSM + 2 * D_SCONV])
    conv, tail = _short_conv(scar_ref[...], g_c * x_b, wsc_ref[...], nb)
    scar_ref[...] = tail
    y_b = g_b * conv

    mixed = jnp.concatenate([y_a.astype(_BF16), y_b.astype(_BF16)], axis=1)
    xo_ref[...] = (x + _dot(mixed, wout_ref[...])).reshape(tt, nb, D_MODEL)

    @pl.when(t_idx == n_t - 1)
    def _():
        ho_ref[...] = hcar_ref[...]
        so_ref[...] = scar_ref[...].reshape(2, nb, D_SCONV)


def _odd_kernel(x_ref, c0_ref, g_ref, w1_ref, b1_ref, wdw_ref, lng_ref, lnb_ref, w2_ref, b2_ref,
                xo_ref, co_ref,
                win_ref, conv_ref, *, tt, nb, n_t):
    t_idx = pl.program_id(1)
    tm = tt * nb
    halo = (CONF_W - 1) * nb

    @pl.when(t_idx == 0)
    def _():
        win_ref[0:halo, :] = c0_ref[...].reshape(halo, D_CONF)

    x = x_ref[...].reshape(tm, D_MODEL)
    xn = _rms(x, g_ref[...]).astype(_BF16)
    for lo in range(0, D_CONF, PW1_SLICE):
        cols = slice(lo, lo + PW1_SLICE)
        gcols = slice(D_CONF + lo, D_CONF + lo + PW1_SLICE)
        a = _dot(xn, w1_ref[:, cols]) + b1_ref[:, cols]
        gate = _dot(xn, w1_ref[:, gcols]) + b1_ref[:, gcols]
        win_ref[halo:halo + tm, cols] = a * _sigmoid(gate)

    def conv_block(r0, n_tiles):
        view = win_ref.at[pl.ds(r0, halo + n_tiles * SUBLANES), :]
        for c in range(D_CONF // LANES):
            lanes = slice(c * LANES, (c + 1) * LANES)
            accs = [None] * n_tiles
            for k in range(CONF_W):
                tap = wdw_ref[k * SUBLANES:(k + 1) * SUBLANES, lanes]
                for j in range(n_tiles):
                    lo = k * nb + j * SUBLANES
                    term = tap * view[lo:lo + SUBLANES, lanes]
                    accs[j] = term if accs[j] is None else accs[j] + term
            for j in range(n_tiles):
                conv_ref[pl.ds(r0 + j * SUBLANES, SUBLANES), lanes] = accs[j]

    n_row_tiles = tm // SUBLANES
    block_rows = CONV_BLOCK_TILES * SUBLANES

    def conv_trip(i, _):
        conv_block(pl.multiple_of(i * block_rows, block_rows), CONV_BLOCK_TILES)
        return 0

    lax.fori_loop(0, n_row_tiles // CONV_BLOCK_TILES, conv_trip, 0)
    if n_row_tiles % CONV_BLOCK_TILES:
        conv_block(n_row_tiles // CONV_BLOCK_TILES * block_rows, n_row_tiles % CONV_BLOCK_TILES)

    conv = conv_ref[...]
    mu = jnp.mean(conv, axis=-1, keepdims=True)
    xc = conv - mu
    yv = xc * lax.rsqrt(jnp.mean(xc * xc, axis=-1, keepdims=True) + EPS)
    yv = yv * lng_ref[...] + lnb_ref[...]
    act = (yv * _sigmoid(yv)).astype(_BF16)

    out = _dot(act, w2_ref[...]) + b2_ref[...]
    xo_ref[...] = (x + out).reshape(tt, nb, D_MODEL)

    @pl.when(t_idx == n_t - 1)
    def _():
        co_ref[...] = win_ref[tm:tm + halo, :].reshape(CONF_W - 1, nb, D_CONF)

    if n_t > 1:
        assert tt >= CONF_W - 1

        @pl.when(t_idx < n_t - 1)
        def _():
            win_ref[0:halo, :] = win_ref[tm:tm + halo, :]


def _ffn_kernel(x_ref, f0_ref, g_ref, wup_ref, wc_ref, wd_ref, gfin_ref,
                xo_ref, fo_ref,
                fcar_ref, hid_ref, *, tt, nb, final_norm):
    t_idx = pl.program_id(1)
    n_t = pl.num_programs(1)
    tm = tt * nb

    @pl.when(t_idx == 0)
    def _():
        fcar_ref[...] = f0_ref[...].reshape(2 * nb, D_FF)

    x = x_ref[...].reshape(tm, D_MODEL)
    xn = _rms(x, g_ref[...]).astype(_BF16)
    for c in range(D_FF // FFN_CHUNK):
        cols = slice(c * FFN_CHUNK, (c + 1) * FFN_CHUNK)
        a = _dot(xn, wup_ref[:, cols])
        gate = _dot(xn, wup_ref[:, D_FF + c * FFN_CHUNK:D_FF + (c + 1) * FFN_CHUNK])
        a_c, tail = _short_conv(fcar_ref[:, cols], a, wc_ref[:, cols], nb)
        fcar_ref[:, cols] = tail
        hid_ref[:, cols] = ((a_c * _sigmoid(a_c)) * gate).astype(_BF16)
    y = x + _dot(hid_ref[...], wd_ref[...])
    if final_norm:
        y = _rms(y, gfin_ref[...])
    xo_ref[...] = y.reshape(tt, nb, D_MODEL)

    @pl.when(t_idx == n_t - 1)
    def _():
        fo_ref[...] = fcar_ref[...].reshape(2, nb, D_FF)


def _layer_spec(arr, layer):
    tail = (0,) * (arr.ndim - 1)
    return pl.BlockSpec((None,) + arr.shape[1:], lambda b, t: (layer,) + tail,
                        pipeline_mode=pl.Buffered(1))


def _x_spec(tt, nbt):
    return pl.BlockSpec((tt, nbt, D_MODEL), lambda b, t: (t, b, 0))


def _state_spec(steps, nbt, width):
    return pl.BlockSpec((steps, nbt, width), lambda b, t: (0, b, 0))


_PARAMS = pltpu.CompilerParams(dimension_semantics=("arbitrary", "arbitrary"),
                               vmem_limit_bytes=VMEM_LIMIT_BYTES)

_EVEN_KEYS = ("w_in", "bcat", "lam", "ccat", "dskip", "w_glu", "b_glu", "w_sconv", "w_out")
_ODD_KEYS = ("w1", "b1", "w_dw", "ln_g", "ln_b", "w2", "b2")
_FFN_KEYS = ("g_ffn", "w_up", "w_conv", "w_down")


def _even_call(x, h0, s0, w, layer, *, tt, nbt, meta=None):
    n_meta = 0 if meta is None else meta.shape[0]
    length, nb = x.shape[0] + n_meta, x.shape[1]
    grid = (nb // nbt, length // tt)
    w_specs = [_layer_spec(w["g_mix"], layer)] + [_layer_spec(w[k], layer // 2) for k in _EVEN_KEYS]
    if n_meta:
        x_specs = [pl.BlockSpec((pl.Element(tt), pl.Element(nbt), pl.Element(D_MODEL)),
                                lambda b, t: (jnp.maximum(t * tt - n_meta, 0), b * nbt, 0)),
                   pl.BlockSpec(meta.shape, lambda b, t: (0, 0))]
        x_args = (x, meta)
    else:
        x_specs, x_args = [_x_spec(tt, nbt)], (x,)
    return pl.pallas_call(
        functools.partial(_even_kernel, tt=tt, nb=nbt, n_meta=n_meta),
        grid=grid,
        in_specs=x_specs + [pl.BlockSpec((nbt, D_STATE), lambda b, t: (b, 0)),
                            _state_spec(SCONV_W - 1, nbt, D_SCONV)] + w_specs,
        out_specs=[_x_spec(tt, nbt),
                   pl.BlockSpec((nbt, D_STATE), lambda b, t: (b, 0)),
                   _state_spec(SCONV_W - 1, nbt, D_SCONV)],
        out_shape=[jax.ShapeDtypeStruct((length, nb, D_MODEL), _F32),
                   jax.ShapeDtypeStruct(h0.shape, _F32),
                   jax.ShapeDtypeStruct(s0.shape, _F32)],
        scratch_shapes=[pltpu.VMEM((nbt, D_STATE), _F32),
                        pltpu.VMEM(((SCONV_W - 1) * nbt, D_SCONV), _F32),
                        pltpu.VMEM((tt * nbt, 2 * KB_STATES), _F32)],
        compiler_params=_PARAMS,
        name="even_mixer",
    )(*x_args, h0, s0, w["g_mix"], *[w[k] for k in _EVEN_KEYS])


def _even_ffn_kernel(*refs, tt, nb, final_norm):
    n_even, n_ffn = 4 + len(_EVEN_KEYS), 2 + len(_FFN_KEYS)
    even_in, ffn_in = refs[:n_even], refs[n_even:n_even + n_ffn]
    (xo_ref, ho_ref, so_ref, fo_ref,
     hcar_ref, scar_ref, hb_ref, fcar_ref, hid_ref, xmid_ref) = refs[n_even + n_ffn:]
    _even_kernel(*even_in, xmid_ref, ho_ref, so_ref, hcar_ref, scar_ref, hb_ref,
                 tt=tt, nb=nb, n_meta=0)
    _ffn_kernel(xmid_ref, *ffn_in, xo_ref, fo_ref, fcar_ref, hid_ref,
                tt=tt, nb=nb, final_norm=final_norm)


def _even_ffn_call(x, h0, s0, f0, w, layer, *, tt, nbt, final_norm):
    length, nb, _ = x.shape
    grid = (nb // nbt, length // tt)
    even_w = [w["g_mix"]] + [w[k] for k in _EVEN_KEYS]
    even_specs = [_layer_spec(w["g_mix"], layer)] + [_layer_spec(w[k], layer // 2) for k in _EVEN_KEYS]
    ffn_w = [w[k] for k in _FFN_KEYS] + [w["g_final"]]
    ffn_specs = [_layer_spec(w[k], layer) for k in _FFN_KEYS] + [_layer_spec(w["g_final"], 0)]
    h_spec = pl.BlockSpec((nbt, D_STATE), lambda b, t: (b, 0))
    return pl.pallas_call(
        functools.partial(_even_ffn_kernel, tt=tt, nb=nbt, final_norm=final_norm),
        grid=grid,
        in_specs=[_x_spec(tt, nbt), h_spec, _state_spec(SCONV_W - 1, nbt, D_SCONV)] + even_specs
        + [_state_spec(FFN_W - 1, nbt, D_FF)] + ffn_specs,
        out_specs=[_x_spec(tt, nbt), h_spec, _state_spec(SCONV_W - 1, nbt, D_SCONV),
                   _state_spec(FFN_W - 1, nbt, D_FF)],
        out_shape=[jax.ShapeDtypeStruct(x.shape, _F32), jax.ShapeDtypeStruct(h0.shape, _F32),
                   jax.ShapeDtypeStruct(s0.shape, _F32), jax.ShapeDtypeStruct(f0.shape, _F32)],
        scratch_shapes=[pltpu.VMEM((nbt, D_STATE), _F32),
                        pltpu.VMEM(((SCONV_W - 1) * nbt, D_SCONV), _F32),
                        pltpu.VMEM((tt * nbt, 2 * KB_STATES), _F32),
                        pltpu.VMEM(((FFN_W - 1) * nbt, D_FF), _F32),
                        pltpu.VMEM((tt * nbt, D_FF), _BF16),
                        pltpu.VMEM((tt, nbt, D_MODEL), _F32)],
        compiler_params=_PARAMS,
        name="even_layer",
    )(x, h0, s0, *even_w, f0, *ffn_w)


def _odd_call(x, c0, w, layer, *, tt, nbt):
    length, nb, _ = x.shape
    n_t = length // tt
    grid = (nb // nbt, n_t)
    w_specs = [_layer_spec(w["g_mix"], layer)] + [_layer_spec(w[k], layer // 2) for k in _ODD_KEYS]
    return pl.pallas_call(
        functools.partial(_odd_kernel, tt=tt, nb=nbt, n_t=n_t),
        grid=grid,
        in_specs=[_x_spec(tt, nbt), _state_spec(CONF_W - 1, nbt, D_CONF)] + w_specs,
        out_specs=[_x_spec(tt, nbt), _state_spec(CONF_W - 1, nbt, D_CONF)],
        out_shape=[jax.ShapeDtypeStruct(x.shape, _F32), jax.ShapeDtypeStruct(c0.shape, _F32)],
        scratch_shapes=[pltpu.VMEM(((CONF_W - 1 + tt) * nbt, D_CONF), _F32),
                        pltpu.VMEM((tt * nbt, D_CONF), _F32)],
        compiler_params=_PARAMS,
        name="odd_mixer",
    )(x, c0, w["g_mix"], *[w[k] for k in _ODD_KEYS])


def _ffn_call(x, f0, w, layer, *, tt, nbt, final_norm):
    length, nb, _ = x.shape
    grid = (nb // nbt, length // tt)
    w_specs = [_layer_spec(w[k], layer) for k in _FFN_KEYS] + [_layer_spec(w["g_final"], 0)]
    return pl.pallas_call(
        functools.partial(_ffn_kernel, tt=tt, nb=nbt, final_norm=final_norm),
        grid=grid,
        in_specs=[_x_spec(tt, nbt), _state_spec(FFN_W - 1, nbt, D_FF)] + w_specs,
        out_specs=[_x_spec(tt, nbt), _state_spec(FFN_W - 1, nbt, D_FF)],
        out_shape=[jax.ShapeDtypeStruct(x.shape, _F32), jax.ShapeDtypeStruct(f0.shape, _F32)],
        scratch_shapes=[pltpu.VMEM(((FFN_W - 1) * nbt, D_FF), _F32),
                        pltpu.VMEM((tt * nbt, D_FF), _BF16)],
        compiler_params=_PARAMS,
        name="conv_ffn",
    )(x, f0, *[w[k] for k in _FFN_KEYS], w["g_final"])


def _rows(v):
    return v.reshape(v.shape[0], 1, v.shape[-1]).astype(_F32)


def _prep_params(p):
    n_even = p["w_in_even"].shape[0]
    p_, s_ = SSM_STATE, SSM_GROUP
    lam = lax.complex(p["ssm_lam_re"].astype(_F32), p["ssm_lam_im"].astype(_F32))
    dt = jnp.exp(p["ssm_log_dt"].astype(_F32))[..., None]
    lam_bar = jnp.exp(lam * dt)
    b_bar = ((lam_bar - 1.0) / lam)[..., None] * lax.complex(
        p["ssm_b_re"].astype(_F32), p["ssm_b_im"].astype(_F32))
    eye = jnp.eye(KB_GROUPS, dtype=_F32)

    def b_block(v):
        v = v.reshape(n_even, N_KB, KB_GROUPS, p_, s_)
        return jnp.einsum("ekgpi,gh->ekgihp", v, eye).reshape(n_even, N_KB, KB_COLS, KB_STATES)

    def c_block(v):
        v = v.reshape(n_even, N_KB, KB_GROUPS, s_, p_)
        return jnp.einsum("ekgip,gh->ekgphi", v, eye).reshape(n_even, N_KB, KB_STATES, KB_COLS)

    bcat = jnp.concatenate([b_block(jnp.real(b_bar)), b_block(jnp.imag(b_bar))], axis=3)
    ccat = jnp.concatenate([c_block(p["ssm_c_re"].astype(_F32)),
                            -c_block(p["ssm_c_im"].astype(_F32))], axis=2)
    lamcat = jnp.stack([jnp.real(lam_bar).reshape(n_even, N_KB, KB_STATES),
                        jnp.imag(lam_bar).reshape(n_even, N_KB, KB_STATES)], axis=2)
    return dict(
        g_mix=_rows(p["norm_mix"]), g_ffn=_rows(p["norm_ffn"]), g_final=_rows(p["norm_final"][None]),
        w_in=p["w_in_even"].astype(_BF16), bcat=bcat.astype(_BF16), lam=lamcat,
        ccat=ccat.astype(_BF16), dskip=_rows(p["ssm_d"]), w_glu=p["w_glu"].astype(_BF16),
        b_glu=_rows(p["b_glu"]), w_sconv=p["w_sconv"].astype(_F32),
        w_out=p["w_out_even"].astype(_BF16),
        w1=p["w_conf_pw1"].astype(_BF16), b1=_rows(p["b_conf_pw1"]),
        w_dw=jnp.repeat(p["w_conf_dw"].astype(_F32), SUBLANES, axis=1),
        ln_g=_rows(p["conf_ln_g"]), ln_b=_rows(p["conf_ln_b"]),
        w2=p["w_conf_pw2"].astype(_BF16), b2=_rows(p["b_conf_pw2"]),
        w_up=p["w_ffn_up"].astype(_BF16), w_conv=p["w_ffn_conv"].astype(_F32),
        w_down=p["w_ffn_down"].astype(_BF16))


def _time_major(v):
    return jnp.transpose(v, (1, 0, 2))


def _ssm_to_flat(re, im):
    nb = re.shape[0]
    re = re.reshape(nb, N_KB, KB_STATES)
    im = im.reshape(nb, N_KB, KB_STATES)
    return jnp.concatenate([re, im], axis=2).reshape(nb, D_STATE).astype(_F32)


def _flat_to_ssm(h):
    nb = h.shape[0]
    h = h.reshape(nb, N_KB, 2, KB_STATES)
    shape = (nb, N_SSM_GROUPS, SSM_STATE)
    return h[:, :, 0].reshape(shape), h[:, :, 1].reshape(shape)


def _trunk(x, ssm_re, ssm_im, sconv, cconv, ffn_buf, w, *, tt, tt_mix, nbt, nbt_odd, meta=None):
    depth = w["g_mix"].shape[0]
    n_re, n_im, n_s, n_c, n_f = [], [], [], [], []
    for l in range(depth):
        i = l // 2
        fused = l % 2 == 0 and tt == tt_mix and meta is None
        if l % 2 == 0:
            h0 = _ssm_to_flat(ssm_re[i], ssm_im[i])
            if fused:
                x, h1, s1, f1 = _even_ffn_call(x, h0, _time_major(sconv[i]), _time_major(ffn_buf[l]),
                                               w, l, tt=tt, nbt=nbt, final_norm=(l == depth - 1))
                n_f.append(_time_major(f1))
            else:
                x, h1, s1 = _even_call(x, h0, _time_major(sconv[i]), w, l, tt=tt_mix, nbt=nbt,
                                       meta=meta if l == 0 else None)
            re, im = _flat_to_ssm(h1)
            n_re.append(re)
            n_im.append(im)
            n_s.append(_time_major(s1))
        else:
            x, c1 = _odd_call(x, _time_major(cconv[i]), w, l, tt=tt_mix, nbt=nbt_odd)
            n_c.append(_time_major(c1))
        if not fused:
            x, f1 = _ffn_call(x, _time_major(ffn_buf[l]), w, l, tt=tt, nbt=nbt,
                              final_norm=(l == depth - 1))
            n_f.append(_time_major(f1))
    return x, jnp.stack(n_re), jnp.stack(n_im), jnp.stack(n_s), jnp.stack(n_c), jnp.stack(n_f)


def kernel(x_prompt, x_sample, state_ssm_re, state_ssm_im, state_sconv, state_cconv, state_ffn, meta_tokens, norm_mix, norm_ffn, norm_final, w_in_even, ssm_lam_re, ssm_lam_im, ssm_log_dt, ssm_b_re, ssm_b_im, ssm_c_re, ssm_c_im, ssm_d, w_glu, b_glu, w_sconv, w_out_even, w_conf_pw1, b_conf_pw1, w_conf_dw, conf_ln_g, conf_ln_b, w_conf_pw2, b_conf_pw2, w_ffn_up, w_ffn_conv, w_ffn_down):
    w = _prep_params(dict(
        norm_mix=norm_mix, norm_ffn=norm_ffn, norm_final=norm_final, w_in_even=w_in_even,
        ssm_lam_re=ssm_lam_re, ssm_lam_im=ssm_lam_im, ssm_log_dt=ssm_log_dt, ssm_b_re=ssm_b_re,
        ssm_b_im=ssm_b_im, ssm_c_re=ssm_c_re, ssm_c_im=ssm_c_im, ssm_d=ssm_d, w_glu=w_glu,
        b_glu=b_glu, w_sconv=w_sconv, w_out_even=w_out_even, w_conf_pw1=w_conf_pw1,
        b_conf_pw1=b_conf_pw1, w_conf_dw=w_conf_dw, conf_ln_g=conf_ln_g, conf_ln_b=conf_ln_b,
        w_conf_pw2=w_conf_pw2, b_conf_pw2=b_conf_pw2, w_ffn_up=w_ffn_up, w_ffn_conv=w_ffn_conv,
        w_ffn_down=w_ffn_down))
    depth = norm_mix.shape[0]
    batch, seq, _ = x_prompt.shape
    dec_batch, dec_seq, _ = x_sample.shape
    dt = x_prompt.dtype
    n_even, n_odd = (depth + 1) // 2, depth // 2

    xp = jnp.transpose(x_prompt, (1, 0, 2))
    zp_h = jnp.zeros((n_even, batch, N_SSM_GROUPS, SSM_STATE), dt)
    zp_s = jnp.zeros((n_even, batch, SCONV_W - 1, D_SCONV), dt)
    zp_c = jnp.zeros((n_odd, batch, CONF_W - 1, D_CONF), dt)
    zp_f = jnp.zeros((depth, batch, FFN_W - 1, D_FF), dt)
    yp, *p_states = _trunk(xp, zp_h, zp_h, zp_s, zp_c, zp_f, w,
                           tt=PROMPT_TILE_STEPS, tt_mix=PROMPT_MIXER_TILE_STEPS, nbt=batch, nbt_odd=batch,
                           meta=meta_tokens.astype(dt))
    y_prompt = jnp.transpose(yp[N_META:], (1, 0, 2))

    xs = jnp.transpose(x_sample, (1, 0, 2))
    ys, *s_states = _trunk(xs, state_ssm_re, state_ssm_im, state_sconv, state_cconv, state_ffn, w,
                           tt=dec_seq, tt_mix=dec_seq, nbt=dec_batch, nbt_odd=SAMPLE_ODD_BATCH_TILE)
    y_sample = jnp.transpose(ys, (1, 0, 2))

    return (y_prompt, y_sample, *p_states, *s_states)
```
